```python
import math
import jax
import jax.numpy as jnp
from jax import lax
import numpy as np

D_MODEL = 2048
BATCH = 1
SEQ = 8192
DEPTH = 4

N_MIXERS = 2
N_ATTN_LAYERS = (DEPTH + 1) // 2
N_CONV_LAYERS = DEPTH // 2
HEAD_DIM = 128
N_HEADS = D_MODEL // HEAD_DIM
N_KV_HEADS = 4
N_REP = N_HEADS // N_KV_HEADS
IDX_HEADS = 16
IDX_DIM = 64
TOPK_MAX = 256
Q_BLOCK = 128
CONV_WIDTH = 3
FFN_HIDDEN = int(math.ceil(8 * D_MODEL / 3 / 256) * 256)
ROPE_THETA = 10000.0
RMS_EPS = 1e-6

Q_COLS = N_HEADS * HEAD_DIM
KV_COLS = N_KV_HEADS * HEAD_DIM
QI_COLS = IDX_HEADS * IDX_DIM
ATTN_IN_COLS = Q_COLS + 2 * KV_COLS + QI_COLS + IDX_DIM + IDX_HEADS
ATTN_SPLITS = (Q_COLS, Q_COLS + KV_COLS, Q_COLS + 2 * KV_COLS,
               Q_COLS + 2 * KV_COLS + QI_COLS, Q_COLS + 2 * KV_COLS + QI_COLS + IDX_DIM)

kernel_name = "hybrid_dsa_shortconv_swiglu"


def _rmsnorm(x, g):
    xf = x.astype(jnp.float32)
    y = xf * lax.rsqrt(jnp.mean(xf * xf, axis=-1, keepdims=True) + RMS_EPS)
    return (y * g.astype(jnp.float32)).astype(x.dtype)


def _rope_tables(seq_len, dim):
    inv_freq = 1.0 / (ROPE_THETA ** (jnp.arange(0, dim, 2, dtype=jnp.float32) / dim))
    ang = jnp.arange(seq_len, dtype=jnp.float32)[:, None] * inv_freq[None, :]
    return jnp.cos(ang)[None, :, None, :], jnp.sin(ang)[None, :, None, :]


def _rope(x, cos, sin):
    xf = x.astype(jnp.float32)
    x1, x2 = jnp.split(xf, 2, axis=-1)
    out = jnp.concatenate([x1 * cos - x2 * sin, x2 * cos + x1 * sin], axis=-1)
    return out.astype(x.dtype)


def _dsa_mixer(h, w_in, q_norm, k_norm, w_out):
    B, S, _ = h.shape
    proj = h @ w_in
    q, k, v, qi, ki, wi = jnp.split(proj, ATTN_SPLITS, axis=-1)
    q = q.reshape(B, S, N_HEADS, HEAD_DIM)
    k = k.reshape(B, S, N_KV_HEADS, HEAD_DIM)
    v = v.reshape(B, S, N_KV_HEADS, HEAD_DIM)
    qi = qi.reshape(B, S, IDX_HEADS, IDX_DIM)
    ki = ki.reshape(B, S, 1, IDX_DIM)

    cos_h, sin_h = _rope_tables(S, HEAD_DIM)
    cos_i, sin_i = _rope_tables(S, IDX_DIM)
    q = _rope(_rmsnorm(q, q_norm), cos_h, sin_h)
    k = _rope(_rmsnorm(k, k_norm), cos_h, sin_h)
    qi = _rope(qi, cos_i, sin_i)
    ki = _rope(ki, cos_i, sin_i)[:, :, 0, :]
    wi = wi * (IDX_HEADS ** -0.5 * IDX_DIM ** -0.5)

    topk = min(TOPK_MAX, S // 4)
    key_pos = jnp.arange(S)
    batch_idx = jnp.arange(B)[:, None, None]
    scale = HEAD_DIM ** -0.5

    def block(qb):
        start = qb * Q_BLOCK
        tq = start + jnp.arange(Q_BLOCK)
        qi_b = lax.dynamic_slice_in_dim(qi, start, Q_BLOCK, axis=1)
        wi_b = lax.dynamic_slice_in_dim(wi, start, Q_BLOCK, axis=1)
        q_b = lax.dynamic_slice_in_dim(q, start, Q_BLOCK, axis=1)
        rel = jax.nn.relu(jnp.einsum('bqhd,bsd->bqhs', qi_b, ki).astype(jnp.float32))
        idx_score = jnp.einsum('bqhs,bqh->bqs', rel, wi_b.astype(jnp.float32))
        causal = key_pos[None, None, :] <= tq[None, :, None]
        idx_score = jnp.where(causal, idx_score, -jnp.inf)
        _, sel = lax.top_k(idx_score, topk)
        k_sel = k[batch_idx, sel]
        v_sel = v[batch_idx, sel]
        q_g = q_b.reshape(B, Q_BLOCK, N_KV_HEADS, N_REP, HEAD_DIM)
        logits = jnp.einsum('bqgrd,bqkgd->bqgrk', q_g, k_sel).astype(jnp.float32) * scale
        valid = (sel <= tq[None, :, None])[:, :, None, None, :]
        p = jax.nn.softmax(jnp.where(valid, logits, -jnp.inf), axis=-1).astype(v.dtype)
        o = jnp.einsum('bqgrk,bqkgd->bqgrd', p, v_sel)
        return o.reshape(B, Q_BLOCK, N_HEADS * HEAD_DIM)

    o = lax.map(block, jnp.arange(S // Q_BLOCK))
    o = jnp.transpose(o, (1, 0, 2, 3)).reshape(B, S, N_HEADS * HEAD_DIM)
    return o @ w_out


def _shortconv_mixer(h, w_in, conv_w, w_out):
    proj = h @ w_in
    b_gate, c_gate, u = jnp.split(proj, 3, axis=-1)
    z = c_gate * u
    zp = jnp.pad(z, ((0, 0), (CONV_WIDTH - 1, 0), (0, 0)))
    y = conv_w[0] * zp[:, :-2] + conv_w[1] * zp[:, 1:-1] + conv_w[2] * zp[:, 2:]
    return (b_gate * y) @ w_out


def _swiglu(h, w_gate, w_up, w_down):
    return (jax.nn.silu(h @ w_gate) * (h @ w_up)) @ w_down


def setup_inputs(seed: int = 0) -> dict:
    key = jax.random.key(seed)
    ks = jax.random.split(key, 16)
    f32 = jnp.float32

    def w(k, shape, fan_in):
        return jax.random.normal(k, shape, f32) * (fan_in ** -0.5)

    def gain(k, shape):
        return 1.0 + 0.02 * jax.random.normal(k, shape, f32)

    na, nc = N_ATTN_LAYERS, N_CONV_LAYERS
    return {
        "x": jax.random.normal(ks[0], (BATCH, SEQ, D_MODEL), f32),
        "attn_norm": gain(ks[1], (na, D_MODEL)),
        "attn_w_in": w(ks[2], (na, D_MODEL, ATTN_IN_COLS), D_MODEL),
        "attn_q_norm": gain(ks[3], (na, HEAD_DIM)),
        "attn_k_norm": gain(ks[4], (na, HEAD_DIM)),
        "attn_w_out": w(ks[5], (na, N_HEADS * HEAD_DIM, D_MODEL), N_HEADS * HEAD_DIM),
        "conv_norm": gain(ks[6], (nc, D_MODEL)),
        "conv_w_in": w(ks[7], (nc, D_MODEL, 3 * D_MODEL), D_MODEL),
        "conv_w": w(ks[8], (nc, CONV_WIDTH, D_MODEL), CONV_WIDTH),
        "conv_w_out": w(ks[9], (nc, D_MODEL, D_MODEL), D_MODEL),
        "ffn_norm": gain(ks[10], (DEPTH, D_MODEL)),
        "ffn_w_gate": w(ks[11], (DEPTH, D_MODEL, FFN_HIDDEN), D_MODEL),
        "ffn_w_up": w(ks[12], (DEPTH, D_MODEL, FFN_HIDDEN), D_MODEL),
        "ffn_w_down": w(ks[13], (DEPTH, FFN_HIDDEN, D_MODEL), FFN_HIDDEN),
    }


def reference(x, attn_norm, attn_w_in, attn_q_norm, attn_k_norm, attn_w_out,
              conv_norm, conv_w_in, conv_w, conv_w_out,
              ffn_norm, ffn_w_gate, ffn_w_up, ffn_w_down):
    for i in range(DEPTH):
        j = i // N_MIXERS
        if i % N_MIXERS == 0:
            x = x + _dsa_mixer(_rmsnorm(x, attn_norm[j]), attn_w_in[j], attn_q_norm[j],
                               attn_k_norm[j], attn_w_out[j])
        else:
            x = x + _shortconv_mixer(_rmsnorm(x, conv_norm[j]), conv_w_in[j], conv_w[j], conv_w_out[j])
        x = x + _swiglu(_rmsnorm(x, ffn_norm[i]), ffn_w_gate[i], ffn_w_up[i], ffn_w_down[i])
    return x
```

```python
import functools

import jax
import jax.numpy as jnp
from jax import lax
from jax.experimental import pallas as pl
from jax.experimental.pallas import tpu as pltpu

HEAD_DIM = 128
N_KV_HEADS = 4
IDX_HEADS = 16
IDX_DIM = 64
TOPK_MAX = 256
CONV_WIDTH = 3
ROPE_THETA = 10000.0
RMS_EPS = 1e-6

LANES = 128
VMEM_LIMIT_BYTES = 56 * 1024 * 1024

ROW_TILE = 512
COL_TILE = 512
Q_TILE = 256
KEY_CHUNK = 512

NEG_BIAS = -1e30
INT_MIN = -2 ** 31

_NT = (((1,), (1,)), ((), ()))

f32 = jnp.float32
bf16 = jnp.bfloat16
i32 = jnp.int32


def _resident(block_shape, index_map):
    return pl.BlockSpec(block_shape, index_map, pipeline_mode=pl.Buffered(1))


def _params(*semantics):
    return pltpu.CompilerParams(dimension_semantics=semantics, vmem_limit_bytes=VMEM_LIMIT_BYTES)


def _rmsnorm_bf16(x, g):
    ms = jnp.mean(x * x, axis=-1, keepdims=True)
    return (x * lax.rsqrt(ms + RMS_EPS) * g).astype(bf16)


def _rope_pairs(y, cos, sin, half):
    if 2 * half == LANES:
        partner = pltpu.roll(y, half, 1)
    else:
        lane = lax.broadcasted_iota(i32, y.shape, 1)
        partner = jnp.where((lane & (2 * half - 1)) < half,
                            pltpu.roll(y, LANES - half, 1), pltpu.roll(y, half, 1))
    return y * cos + partner * sin


def _attn_proj_kernel(x_ref, g_ref, w_ref, wvt_ref, qg_ref, kg_ref, cosh_ref, sinh_ref,
                      cosi_ref, sini_ref,
                      q_ref, k_ref, vt_ref, qi_ref, kia_ref, kib_ref, wi_ref,
                      *, q_cols, kv_cols, qi_cols, q_scale, wi_scale):
    h = _rmsnorm_bf16(x_ref[...], g_ref[...])
    cosh, sinh = cosh_ref[...], sinh_ref[...]
    cosi, sini = cosi_ref[...], sini_ref[...]

    def head_norm_rope(acc, gain, scale, out_ref, col0):
        for t in range(acc.shape[1] // HEAD_DIM):
            y = acc[:, t * HEAD_DIM:(t + 1) * HEAD_DIM]
            ms = jnp.mean(y * y, axis=-1, keepdims=True)
            y = y * lax.rsqrt(ms + RMS_EPS) * gain
            y = _rope_pairs(y, cosh, sinh, HEAD_DIM // 2)
            if scale != 1.0:
                y = y * scale
            out_ref[:, col0 + t * HEAD_DIM:col0 + (t + 1) * HEAD_DIM] = y.astype(out_ref.dtype)

    col = 0
    for c0 in range(0, q_cols, COL_TILE):
        acc = jnp.dot(h, w_ref[:, col + c0:col + c0 + COL_TILE], preferred_element_type=f32)
        head_norm_rope(acc, qg_ref[...], q_scale, q_ref, c0)
    col += q_cols
    for c0 in range(0, kv_cols, COL_TILE):
        acc = jnp.dot(h, w_ref[:, col + c0:col + c0 + COL_TILE], preferred_element_type=f32)
        head_norm_rope(acc, kg_ref[...], 1.0, k_ref, c0)
    col += kv_cols
    vt_ref[0] = lax.dot_general(wvt_ref[...], h, _NT, preferred_element_type=f32).astype(vt_ref.dtype)
    for c0 in range(0, qi_cols, COL_TILE):
        acc = jnp.dot(h, w_ref[:, col + c0:col + c0 + COL_TILE], preferred_element_type=f32)
        for t in range(COL_TILE // LANES):
            y = _rope_pairs(acc[:, t * LANES:(t + 1) * LANES], cosi, sini, IDX_DIM // 2)
            qi_ref[:, c0 + t * LANES:c0 + (t + 1) * LANES] = y.astype(qi_ref.dtype)
    col += qi_cols
    acc = jnp.dot(h, w_ref[:, col:col + LANES], preferred_element_type=f32)
    ki = _rope_pairs(acc, cosi, sini, IDX_DIM // 2)
    lane = lax.broadcasted_iota(i32, acc.shape, 1)
    kia_ref[...] = jnp.where(lane < IDX_DIM, ki, 0.0).astype(kia_ref.dtype)
    kib_ref[...] = jnp.where(lane >= IDX_DIM, pltpu.roll(ki, IDX_DIM, 1), 0.0).astype(kib_ref.dtype)
    wi_ref[...] = acc * wi_scale


def _attn_proj(x, g, w_main, wvt, qg, kg, tables, *, q_cols, kv_cols, qi_cols):
    S, D = x.shape
    tm = KEY_CHUNK
    cosh, sinh, cosi, sini = tables
    n_main = w_main.shape[1]
    row = lambda i: (i, 0)
    const = lambda i: (0, 0)
    kern = functools.partial(_attn_proj_kernel, q_cols=q_cols, kv_cols=kv_cols, qi_cols=qi_cols,
                             q_scale=HEAD_DIM ** -0.5,
                             wi_scale=IDX_HEADS ** -0.5 * IDX_DIM ** -0.5)
    return pl.pallas_call(
        kern,
        grid=(S // tm,),
        in_specs=[
            pl.BlockSpec((tm, D), row),
            _resident((1, D), const),
            _resident((D, n_main), const),
            _resident((kv_cols, D), const),
            _resident((1, HEAD_DIM), const),
            _resident((1, HEAD_DIM), const),
            pl.BlockSpec((tm, LANES), row), pl.BlockSpec((tm, LANES), row),
            pl.BlockSpec((tm, LANES), row), pl.BlockSpec((tm, LANES), row),
        ],
        out_specs=[
            pl.BlockSpec((tm, q_cols), row),
            pl.BlockSpec((tm, kv_cols), row),
            pl.BlockSpec((1, kv_cols, tm), lambda i: (i, 0, 0)),
            pl.BlockSpec((tm, qi_cols), row),
            pl.BlockSpec((tm, LANES), row),
            pl.BlockSpec((tm, LANES), row),
            pl.BlockSpec((tm, LANES), row),
        ],
        out_shape=[
            jax.ShapeDtypeStruct((S, q_cols), bf16),
            jax.ShapeDtypeStruct((S, kv_cols), bf16),
            jax.ShapeDtypeStruct((S // tm, kv_cols, tm), bf16),
            jax.ShapeDtypeStruct((S, qi_cols), bf16),
            jax.ShapeDtypeStruct((S, LANES), bf16),
            jax.ShapeDtypeStruct((S, LANES), bf16),
            jax.ShapeDtypeStruct((S, LANES), f32),
        ],
        compiler_params=_params("parallel"),
        name="attn_proj",
    )(x, g, w_main, wvt, qg, kg, cosh, sinh, cosi, sini)


def _dsa_attention_kernel(q_ref, qi_ref, wit_ref, k_ref, vt_ref, kia_ref, kib_ref, o_ref,
                          key_ref, bias_ref, *, topk, n_heads, idx_bits):
    tq, kc = Q_TILE, KEY_CHUNK
    q0 = pl.program_id(0) * tq
    n_chunks = lax.div(q0 + tq + kc - 1, kc)
    qpos = q0 + lax.broadcasted_iota(i32, (1, tq), 1)
    n_rep = n_heads // N_KV_HEADS
    neg_inf_key = jnp.int32(-2139095041)

    def score_chunk(c, carry):
        r0 = pl.multiple_of(c * kc, kc)
        ka = kia_ref[pl.ds(r0, kc), :]
        kb = kib_ref[pl.ds(r0, kc), :]
        acc = jnp.zeros((kc, tq), f32)
        for p in range(IDX_HEADS // 2):
            rhs = qi_ref[:, p * LANES:(p + 1) * LANES]
            sa = lax.dot_general(ka, rhs, _NT, preferred_element_type=f32)
            sb = lax.dot_general(kb, rhs, _NT, preferred_element_type=f32)
            acc = acc + jnp.maximum(sa, 0.0) * wit_ref[2 * p:2 * p + 1, :]
            acc = acc + jnp.maximum(sb, 0.0) * wit_ref[2 * p + 1:2 * p + 2, :]
        kpos = r0 + lax.broadcasted_iota(i32, (kc, 1), 0)
        acc = jnp.where(kpos <= qpos, acc, -jnp.inf)
        bits = pltpu.bitcast(acc, i32)
        key_ref[pl.ds(r0, kc), :] = bits ^ ((bits >> 31) & jnp.int32(0x7FFFFFFF))
        return carry

    lax.fori_loop(0, n_chunks, score_chunk, 0)

    def count_where(pred):
        def body(c, cnt):
            r0 = pl.multiple_of(c * kc, kc)
            hit = jnp.where(pred(key_ref[pl.ds(r0, kc), :], r0), 1, 0).astype(i32)
            return cnt + hit.reshape(kc // 8, 8, tq).sum(axis=0)
        cnt = lax.fori_loop(0, n_chunks, body, jnp.zeros((8, tq), i32))
        return cnt.sum(axis=0, keepdims=True)

    def value_bit(it, carry):
        thr, cnt_ge = carry
        cand = thr ^ jnp.left_shift(jnp.int32(1), 31 - it)
        tot = count_where(lambda blk, r0: blk >= cand)
        take = tot >= topk
        return jnp.where(take, cand, thr), jnp.where(take, tot, cnt_ge)

    thr, cnt_ge = lax.fori_loop(
        0, 32, value_bit,
        (jnp.full((1, tq), INT_MIN, i32), jnp.full((1, tq), n_chunks * kc, i32)))

    def no_ties():
        return jnp.full((1, tq), jnp.iinfo(jnp.int32).max, i32)

    def break_ties():
        n_eq = count_where(lambda blk, r0: blk == thr)
        want = topk - (cnt_ge - n_eq)

        def index_bit(it, cut):
            cand = cut | jnp.left_shift(jnp.int32(1), idx_bits - 1 - it)

            def pred(blk, r0):
                kpos = r0 + lax.broadcasted_iota(i32, (kc, 1), 0)
                return (blk == thr) & (kpos < cand)
            return jnp.where(count_where(pred) < want, cand, cut)

        return lax.fori_loop(0, idx_bits, index_bit, jnp.zeros((1, tq), i32))

    idx_cut = lax.cond(jnp.max(cnt_ge) > topk, break_ties, no_ties)

    def bias_chunk(c, carry):
        r0 = pl.multiple_of(c * kc, kc)
        blk = key_ref[pl.ds(r0, kc), :]
        kpos = r0 + lax.broadcasted_iota(i32, (kc, 1), 0)
        sel = ((blk > thr) | ((blk == thr) & (kpos <= idx_cut))) & (blk > neg_inf_key)
        bias_ref[pl.ds(r0, kc), :] = jnp.where(sel, 0.0, NEG_BIAS).astype(f32)
        return carry

    lax.fori_loop(0, n_chunks, bias_chunk, 0)

    for h in range(n_heads):
        g = h // n_rep
        q_h = q_ref[:, h * HEAD_DIM:(h + 1) * HEAD_DIM]

        def attend_chunk(c, carry, g=g, q_h=q_h):
            m, l, acc = carry
            r0 = pl.multiple_of(c * kc, kc)
            k_c = k_ref[pl.ds(r0, kc), g * HEAD_DIM:(g + 1) * HEAD_DIM]
            s = lax.dot_general(k_c, q_h, _NT, preferred_element_type=f32)
            s = s + bias_ref[pl.ds(r0, kc), :]
            m_new = jnp.maximum(m, s.max(axis=0, keepdims=True))
            alpha = jnp.exp(m - m_new)
            p = jnp.exp(s - m_new)
            l = alpha * l + p.sum(axis=0, keepdims=True)
            vt_c = vt_ref[c, g * HEAD_DIM:(g + 1) * HEAD_DIM, :]
            acc = alpha * acc + jnp.dot(vt_c, p.astype(bf16), preferred_element_type=f32)
            return m_new, l, acc

        m, l, acc = lax.fori_loop(
            0, n_chunks, attend_chunk,
            (jnp.full((1, tq), NEG_BIAS, f32), jnp.zeros((1, tq), f32),
             jnp.zeros((HEAD_DIM, tq), f32)))
        o_t = acc * (1.0 / l)
        o_ref[:, h * HEAD_DIM:(h + 1) * HEAD_DIM] = o_t.T.astype(o_ref.dtype)


def _dsa_attention(q, qi, wit, k, vt, kia, kib, *, topk):
    S, q_cols = q.shape
    n_heads = q_cols // HEAD_DIM
    kv_cols = k.shape[1]
    tq = Q_TILE
    row = lambda i: (i, 0)
    const2 = lambda i: (0, 0)
    kern = functools.partial(_dsa_attention_kernel, topk=topk, n_heads=n_heads,
                             idx_bits=max(1, (S - 1).bit_length()))
    return pl.pallas_call(
        kern,
        grid=(S // tq,),
        in_specs=[
            pl.BlockSpec((tq, q_cols), row),
            pl.BlockSpec((tq, qi.shape[1]), row),
            pl.BlockSpec((IDX_HEADS, tq), lambda i: (0, i)),
            _resident((S, kv_cols), const2),
            _resident(vt.shape, lambda i: (0, 0, 0)),
            _resident((S, LANES), const2),
            _resident((S, LANES), const2),
        ],
        out_specs=pl.BlockSpec((tq, q_cols), row),
        out_shape=jax.ShapeDtypeStruct((S, q_cols), bf16),
        scratch_shapes=[pltpu.VMEM((S, tq), i32), pltpu.VMEM((S, tq), f32)],
        compiler_params=_params("parallel"),
        name="dsa_attention",
    )(q, qi, wit, k, vt, kia, kib)


def _matmul_residual_kernel(a_ref, w_ref, res_ref, o_ref):
    a = a_ref[...]
    for c0 in range(0, o_ref.shape[1], COL_TILE):
        acc = jnp.dot(a, w_ref[:, c0:c0 + COL_TILE], preferred_element_type=f32)
        o_ref[:, c0:c0 + COL_TILE] = res_ref[:, c0:c0 + COL_TILE] + acc


def _matmul_residual(a, w, res):
    S, K = a.shape
    N = w.shape[1]
    tm = ROW_TILE
    row = lambda i: (i, 0)
    return pl.pallas_call(
        _matmul_residual_kernel,
        grid=(S // tm,),
        in_specs=[pl.BlockSpec((tm, K), row), _resident((K, N), lambda i: (0, 0)),
                  pl.BlockSpec((tm, N), row)],
        out_specs=pl.BlockSpec((tm, N), row),
        out_shape=jax.ShapeDtypeStruct((S, N), f32),
        compiler_params=_params("parallel"),
        name="matmul_residual",
    )(a, w, res)


def _conv_proj_kernel(x_ref, g_ref, w_ref, cw_ref, a_ref, tail_ref):
    tm, D = x_ref.shape

    @pl.when(pl.program_id(0) == 0)
    def _():
        tail_ref[...] = jnp.zeros_like(tail_ref)

    h = _rmsnorm_bf16(x_ref[...], g_ref[...])
    rowi = lax.broadcasted_iota(i32, (tm, COL_TILE), 0)
    for c0 in range(0, D, COL_TILE):
        cols = slice(c0, c0 + COL_TILE)
        b = jnp.dot(h, w_ref[:, c0:c0 + COL_TILE], preferred_element_type=f32)
        c = jnp.dot(h, w_ref[:, D + c0:D + c0 + COL_TILE], preferred_element_type=f32)
        u = jnp.dot(h, w_ref[:, 2 * D + c0:2 * D + c0 + COL_TILE], preferred_element_type=f32)
        z = c * u
        prev1 = tail_ref[7:8, cols]
        prev2 = tail_ref[6:7, cols]
        z1 = jnp.where(rowi == 0, prev1, pltpu.roll(z, 1, 0))
        z2 = jnp.where(rowi == 0, prev2, jnp.where(rowi == 1, prev1, pltpu.roll(z, 2, 0)))
        y = cw_ref[0:1, cols] * z2 + cw_ref[1:2, cols] * z1 + cw_ref[2:3, cols] * z
        a_ref[:, cols] = (b * y).astype(a_ref.dtype)
        tail_ref[:, cols] = z[tm - 8:tm, :]


def _conv_proj(x, g, w_in, conv_w):
    S, D = x.shape
    tm = ROW_TILE
    row = lambda i: (i, 0)
    const = lambda i: (0, 0)
    return pl.pallas_call(
        _conv_proj_kernel,
        grid=(S // tm,),
        in_specs=[pl.BlockSpec((tm, D), row), _resident((1, D), const),
                  _resident((D, 3 * D), const), _resident((CONV_WIDTH, D), const)],
        out_specs=pl.BlockSpec((tm, D), row),
        out_shape=jax.ShapeDtypeStruct((S, D), bf16),
        scratch_shapes=[pltpu.VMEM((8, D), f32)],
        compiler_params=_params("arbitrary"),
        name="conv_proj",
    )(x, g, w_in, conv_w)


def _ffn_kernel(x_ref, g_ref, wg_ref, wu_ref, wd_ref, o_ref, h_ref, acc_ref):
    j = pl.program_id(1)

    @pl.when(j == 0)
    def _():
        h_ref[...] = _rmsnorm_bf16(x_ref[...], g_ref[...])
        acc_ref[...] = jnp.zeros_like(acc_ref)

    h = h_ref[...]
    gate = jnp.dot(h, wg_ref[...], preferred_element_type=f32)
    up = jnp.dot(h, wu_ref[...], preferred_element_type=f32)
    act = (gate * jax.nn.sigmoid(gate) * up).astype(bf16)
    acc_ref[...] += jnp.dot(act, wd_ref[...], preferred_element_type=f32)

    @pl.when(j == pl.num_programs(1) - 1)
    def _():
        o_ref[...] = x_ref[...] + acc_ref[...]


def _ffn(x, g, wg, wu, wd):
    S, D = x.shape
    H = wg.shape[1]
    tm, th = ROW_TILE, COL_TILE
    return pl.pallas_call(
        _ffn_kernel,
        grid=(S // tm, H // th),
        in_specs=[pl.BlockSpec((tm, D), lambda i, j: (i, 0)),
                  _resident((1, D), lambda i, j: (0, 0)),
                  pl.BlockSpec((D, th), lambda i, j: (0, j)),
                  pl.BlockSpec((D, th), lambda i, j: (0, j)),
                  pl.BlockSpec((th, D), lambda i, j: (j, 0))],
        out_specs=pl.BlockSpec((tm, D), lambda i, j: (i, 0)),
        out_shape=jax.ShapeDtypeStruct((S, D), f32),
        scratch_shapes=[pltpu.VMEM((tm, D), bf16), pltpu.VMEM((tm, D), f32)],
        compiler_params=_params("parallel", "arbitrary"),
        name="ffn",
    )(x, g, wg, wu, wd)


def _rope_tables(seq_len):
    def angles(dim):
        inv_freq = 1.0 / (ROPE_THETA ** (jnp.arange(0, dim, 2, dtype=f32) / dim))
        return jnp.arange(seq_len, dtype=f32)[:, None] * inv_freq[None, :]

    ah, ai = angles(HEAD_DIM), angles(IDX_DIM)
    cosh = jnp.concatenate([jnp.cos(ah)] * 2, axis=-1)
    sinh = jnp.concatenate([-jnp.sin(ah), jnp.sin(ah)], axis=-1)
    cosi = jnp.concatenate([jnp.cos(ai)] * 4, axis=-1)
    sini = jnp.concatenate([-jnp.sin(ai), jnp.sin(ai)] * 2, axis=-1)
    return cosh, sinh, cosi, sini


def _attention_layer(x, norm_g, w_in, q_norm, k_norm, w_out, tables):
    S, D = x.shape
    q_cols = D
    kv_cols = N_KV_HEADS * HEAD_DIM
    qi_cols = IDX_HEADS * IDX_DIM
    v0 = q_cols + kv_cols
    qi0 = q_cols + 2 * kv_cols
    tail = w_in[:, qi0 + qi_cols:]
    w_main = jnp.concatenate(
        [w_in[:, :v0], w_in[:, qi0:qi0 + qi_cols],
         jnp.pad(tail, ((0, 0), (0, LANES - tail.shape[1])))], axis=1).astype(bf16)
    wvt = w_in[:, v0:qi0].T.astype(bf16)
    q, k, vt, qi, kia, kib, wi = _attn_proj(
        x, norm_g[None, :], w_main, wvt, q_norm[None, :], k_norm[None, :], tables,
        q_cols=q_cols, kv_cols=kv_cols, qi_cols=qi_cols)
    wit = wi[:, IDX_DIM:IDX_DIM + IDX_HEADS].T
    o = _dsa_attention(q, qi, wit, k, vt, kia, kib, topk=min(TOPK_MAX, S // 4))
    return _matmul_residual(o, w_out.astype(bf16), x)


def _conv_layer(x, norm_g, w_in, conv_w, w_out):
    a = _conv_proj(x, norm_g[None, :], w_in.astype(bf16), conv_w)
    return _matmul_residual(a, w_out.astype(bf16), x)


def kernel(x, attn_norm, attn_w_in, attn_q_norm, attn_k_norm, attn_w_out, conv_norm, conv_w_in,
           conv_w, conv_w_out, ffn_norm, ffn_w_gate, ffn_w_up, ffn_w_down):
    B, S, D = x.shape
    assert B == 1 and S % KEY_CHUNK == 0 and D % COL_TILE == 0
    depth = ffn_norm.shape[0]
    tables = _rope_tables(S)
    xs = x[0]
    for i in range(depth):
        j = i // 2
        if i % 2 == 0:
            xs = _attention_layer(xs, attn_norm[j], attn_w_in[j], attn_q_norm[j], attn_k_norm[j],
                                  attn_w_out[j], tables)
        else:
            xs = _conv_layer(xs, conv_norm[j], conv_w_in[j], conv_w[j], conv_w_out[j])
        xs = _ffn(xs, ffn_norm[i][None, :], ffn_w_gate[i].astype(bf16), ffn_w_up[i].astype(bf16),
                  ffn_w_down[i].astype(bf16))
    return xs[None]
```

```python
import functools

import jax
import jax.numpy as jnp
from jax import lax
from jax.experimental import pallas as pl
from jax.experimental.pallas import tpu as pltpu

HEAD_DIM = 128
N_KV_HEADS = 4
IDX_HEADS = 16
IDX_DIM = 64
TOPK_MAX = 256
CONV_WIDTH = 3
ROPE_THETA = 10000.0
RMS_EPS = 1e-6

LANES = 128
VMEM_LIMIT_BYTES = 56 * 1024 * 1024

ROW_TILE = 512
COL_TILE = 512
Q_TILE = 256
KEY_CHUNK = 512
COUNT_ROWS = 64

NEG_BIAS = -1e30
LOG2_E = 1.4426950408889634
BOUND_SLACK = 1.01
DENOM_MIN = 2.0 ** -100
I16_MIN, I16_MAX = -2 ** 15, 2 ** 15 - 1

_NT = (((1,), (1,)), ((), ()))

f32 = jnp.float32
bf16 = jnp.bfloat16
i32 = jnp.int32
i16 = jnp.int16


def _resident(block_shape, index_map):
    return pl.BlockSpec(block_shape, index_map, pipeline_mode=pl.Buffered(1))


def _params(*semantics):
    return pltpu.CompilerParams(dimension_semantics=semantics, vmem_limit_bytes=VMEM_LIMIT_BYTES)


def _rmsnorm_bf16(x, g):
    ms = jnp.mean(x * x, axis=-1, keepdims=True)
    return (x * lax.rsqrt(ms + RMS_EPS) * g).astype(bf16)


def _rope_pairs(y, cos, sin, half):
    if 2 * half == LANES:
        partner = pltpu.roll(y, half, 1)
    else:
        lane = lax.broadcasted_iota(i32, y.shape, 1)
        partner = jnp.where((lane & (2 * half - 1)) < half,
                            pltpu.roll(y, LANES - half, 1), pltpu.roll(y, half, 1))
    return y * cos + partner * sin


def _attn_proj_kernel(x_ref, g_ref, w_ref, qg_ref, kg_ref, cosh_ref, sinh_ref,
                      cosi_ref, sini_ref,
                      q_ref, k_ref, v_ref, qi_ref, kia_ref, kib_ref, wi_ref,
                      *, q_cols, kv_cols, qi_cols, q_scale, wi_scale):
    h = _rmsnorm_bf16(x_ref[...], g_ref[...])
    cosh, sinh = cosh_ref[...], sinh_ref[...]
    cosi, sini = cosi_ref[...], sini_ref[...]

    def head_norm_rope(acc, gain, scale, out_ref, col0):
        for t in range(acc.shape[1] // HEAD_DIM):
            y = acc[:, t * HEAD_DIM:(t + 1) * HEAD_DIM]
            ms = jnp.mean(y * y, axis=-1, keepdims=True)
            y = y * lax.rsqrt(ms + RMS_EPS) * gain
            y = _rope_pairs(y, cosh, sinh, HEAD_DIM // 2)
            if scale != 1.0:
                y = y * scale
            out_ref[:, col0 + t * HEAD_DIM:col0 + (t + 1) * HEAD_DIM] = y.astype(out_ref.dtype)

    col = 0
    for c0 in range(0, q_cols, COL_TILE):
        acc = jnp.dot(h, w_ref[:, col + c0:col + c0 + COL_TILE], preferred_element_type=f32)
        head_norm_rope(acc, qg_ref[...], q_scale, q_ref, c0)
    col += q_cols
    for c0 in range(0, kv_cols, COL_TILE):
        acc = jnp.dot(h, w_ref[:, col + c0:col + c0 + COL_TILE], preferred_element_type=f32)
        head_norm_rope(acc, kg_ref[...], 1.0, k_ref, c0)
    col += kv_cols
    for c0 in range(0, kv_cols, COL_TILE):
        acc = jnp.dot(h, w_ref[:, col + c0:col + c0 + COL_TILE], preferred_element_type=f32)
        v_ref[:, c0:c0 + COL_TILE] = acc.astype(v_ref.dtype)
    col += kv_cols
    for c0 in range(0, qi_cols, COL_TILE):
        acc = jnp.dot(h, w_ref[:, col + c0:col + c0 + COL_TILE], preferred_element_type=f32)
        for t in range(COL_TILE // LANES):
            y = _rope_pairs(acc[:, t * LANES:(t + 1) * LANES], cosi, sini, IDX_DIM // 2)
            qi_ref[:, c0 + t * LANES:c0 + (t + 1) * LANES] = y.astype(qi_ref.dtype)
    col += qi_cols
    acc = jnp.dot(h, w_ref[:, col:col + LANES], preferred_element_type=f32)
    ki = _rope_pairs(acc, cosi, sini, IDX_DIM // 2)
    lane = lax.broadcasted_iota(i32, acc.shape, 1)
    kia_ref[...] = jnp.where(lane < IDX_DIM, ki, 0.0).astype(kia_ref.dtype)
    kib_ref[...] = jnp.where(lane >= IDX_DIM, pltpu.roll(ki, IDX_DIM, 1), 0.0).astype(kib_ref.dtype)
    wi_ref[...] = acc * wi_scale


def _attn_proj(x, g, w_all, qg, kg, tables, *, q_cols, kv_cols, qi_cols):
    S, D = x.shape
    tm = ROW_TILE
    cosh, sinh, cosi, sini = tables
    row = lambda i: (i, 0)
    const = lambda i: (0, 0)
    kern = functools.partial(_attn_proj_kernel, q_cols=q_cols, kv_cols=kv_cols, qi_cols=qi_cols,
                             q_scale=HEAD_DIM ** -0.5 * LOG2_E,
                             wi_scale=IDX_HEADS ** -0.5 * IDX_DIM ** -0.5)
    return pl.pallas_call(
        kern,
        grid=(S // tm,),
        in_specs=[
            pl.BlockSpec((tm, D), row),
            _resident((1, D), const),
            _resident((D, w_all.shape[1]), const),
            _resident((1, HEAD_DIM), const),
            _resident((1, HEAD_DIM), const),
            pl.BlockSpec((tm, LANES), row), pl.BlockSpec((tm, LANES), row),
            pl.BlockSpec((tm, LANES), row), pl.BlockSpec((tm, LANES), row),
        ],
        out_specs=[
            pl.BlockSpec((tm, q_cols), row),
            pl.BlockSpec((tm, kv_cols), row),
            pl.BlockSpec((tm, kv_cols), row),
            pl.BlockSpec((tm, qi_cols), row),
            pl.BlockSpec((tm, LANES), row),
            pl.BlockSpec((tm, LANES), row),
            pl.BlockSpec((tm, LANES), row),
        ],
        out_shape=[
            jax.ShapeDtypeStruct((S, q_cols), bf16),
            jax.ShapeDtypeStruct((S, kv_cols), bf16),
            jax.ShapeDtypeStruct((S, kv_cols), bf16),
            jax.ShapeDtypeStruct((S, qi_cols), bf16),
            jax.ShapeDtypeStruct((S, LANES), bf16),
            jax.ShapeDtypeStruct((S, LANES), bf16),
            jax.ShapeDtypeStruct((S, LANES), f32),
        ],
        compiler_params=_params("parallel"),
        name="attn_proj",
    )(x, g, w_all, qg, kg, cosh, sinh, cosi, sini)


def _dsa_attention_kernel(q_ref, qi_ref, wit_ref, k_ref, v_ref, kia_ref, kib_ref, o_ref,
                          key_ref, hi_ref, lo_ref, qcat_ref, acc_ref, kmax_ref,
                          *, topk, n_heads, idx_bits):
    tq, kc = Q_TILE, KEY_CHUNK
    q0 = pl.program_id(0) * tq
    n_chunks = lax.div(q0 + tq + kc - 1, kc)
    qpos = q0 + lax.broadcasted_iota(i32, (1, tq), 1)
    n_rep = n_heads // N_KV_HEADS
    neg_inf_key = jnp.int32(-2139095041)

    def score_chunk(c, carry):
        r0 = pl.multiple_of(c * kc, kc)
        ka = kia_ref[pl.ds(r0, kc), :]
        kb = kib_ref[pl.ds(r0, kc), :]
        acc = jnp.zeros((kc, tq), f32)
        for p in range(IDX_HEADS // 2):
            rhs = qi_ref[:, p * LANES:(p + 1) * LANES]
            sa = lax.dot_general(ka, rhs, _NT, preferred_element_type=f32)
            sb = lax.dot_general(kb, rhs, _NT, preferred_element_type=f32)
            acc = acc + jnp.maximum(sa, 0.0) * wit_ref[2 * p:2 * p + 1, :]
            acc = acc + jnp.maximum(sb, 0.0) * wit_ref[2 * p + 1:2 * p + 2, :]
        kpos = r0 + lax.broadcasted_iota(i32, (kc, 1), 0)
        acc = jnp.where(kpos <= qpos, acc, -jnp.inf)
        bits = pltpu.bitcast(acc, i32)
        key = bits ^ ((bits >> 31) & jnp.int32(0x7FFFFFFF))
        key_ref[pl.ds(r0, kc), :] = key
        hi_ref[pl.ds(r0, kc), :] = (key >> 16).astype(i16)
        lo_ref[pl.ds(r0, kc), :] = ((key & 0xFFFF) + I16_MIN).astype(i16)
        return carry

    lax.fori_loop(0, n_chunks, score_chunk, 0)

    def count_where(pred):
        def body(c, cnt):
            r0 = pl.multiple_of(c * kc, kc)
            hit = jnp.where(pred(key_ref[pl.ds(r0, kc), :], r0), 1, 0).astype(i32)
            return cnt + hit.reshape(kc // COUNT_ROWS, COUNT_ROWS, tq).sum(axis=0)
        cnt = lax.fori_loop(0, n_chunks, body, jnp.zeros((COUNT_ROWS, tq), i32))
        return cnt.sum(axis=0, keepdims=True)

    def count_ge16(ref, cand):
        cand16 = cand.astype(i16)

        def body(c, cnt):
            r0 = pl.multiple_of(c * kc, kc)
            hit = jnp.where(ref[pl.ds(r0, kc), :] >= cand16, jnp.int16(1), jnp.int16(0))
            for t in range(kc // COUNT_ROWS):
                cnt = cnt + hit[t * COUNT_ROWS:(t + 1) * COUNT_ROWS]
            return cnt
        cnt = lax.fori_loop(0, n_chunks, body, jnp.zeros((COUNT_ROWS, tq), i16))
        return cnt.astype(i32).sum(axis=0, keepdims=True)

    def search16(ref, cnt_all):
        def bit(it, carry):
            t, cnt_ge = carry
            cand = t ^ jnp.left_shift(jnp.int32(1), 15 - it)
            cand = jnp.where(it == 0, jnp.zeros_like(cand), cand)
            tot = count_ge16(ref, cand)
            take = tot >= topk
            return jnp.where(take, cand, t), jnp.where(take, tot, cnt_ge)
        return lax.fori_loop(0, 16, bit, (jnp.full((1, tq), I16_MIN, i32), cnt_all))

    prefix, cnt_hi = search16(hi_ref, jnp.full((1, tq), n_chunks * kc, i32))
    prefix16 = prefix.astype(i16)

    def pin_chunk(c, carry):
        r0 = pl.multiple_of(c * kc, kc)
        hi = hi_ref[pl.ds(r0, kc), :]
        lo = lo_ref[pl.ds(r0, kc), :]
        lo_ref[pl.ds(r0, kc), :] = jnp.where(hi > prefix16, jnp.int16(I16_MAX),
                                             jnp.where(hi < prefix16, jnp.int16(I16_MIN), lo))
        return carry

    lax.fori_loop(0, n_chunks, pin_chunk, 0)
    low, cnt_ge = search16(lo_ref, cnt_hi)
    thr = prefix * 65536 + (low - I16_MIN)

    def no_ties():
        return jnp.full((1, tq), jnp.iinfo(jnp.int32).max, i32)

    def break_ties():
        n_eq = count_where(lambda blk, r0: blk == thr)
        want = topk - (cnt_ge - n_eq)

        def index_bit(it, cut):
            cand = cut | jnp.left_shift(jnp.int32(1), idx_bits - 1 - it)

            def pred(blk, r0):
                kpos = r0 + lax.broadcasted_iota(i32, (kc, 1), 0)
                return (blk == thr) & (kpos < cand)
            return jnp.where(count_where(pred) < want, cand, cut)

        return lax.fori_loop(0, idx_bits, index_bit, jnp.zeros((1, tq), i32))

    idx_cut = lax.cond(jnp.max(cnt_ge) > topk, break_ties, no_ties)

    @pl.when(pl.program_id(0) == 0)
    def _():
        def knorm_chunk(c, best):
            r0 = pl.multiple_of(c * kc, kc)
            kf = k_ref[pl.ds(r0, kc), :].astype(f32)
            sq = kf * kf
            for g in range(N_KV_HEADS):
                n2 = jnp.sum(sq[:, g * HEAD_DIM:(g + 1) * HEAD_DIM], axis=1, keepdims=True)
                best = jnp.maximum(best, jnp.max(n2, axis=0, keepdims=True))
            return best

        best = lax.fori_loop(0, k_ref.shape[0] // kc, knorm_chunk, jnp.zeros((1, 1), f32))
        kmax_ref[...] = jnp.broadcast_to(best, kmax_ref.shape)

    ones = jnp.ones((8, HEAD_DIM), f32)
    qn2 = jnp.zeros((8, tq), f32)
    for h in range(n_heads):
        qf = q_ref[:, h * HEAD_DIM:(h + 1) * HEAD_DIM].astype(f32)
        qn2 = jnp.maximum(qn2, lax.dot_general(ones, qf * qf, _NT, preferred_element_type=f32))
    m_fix = jnp.sqrt(qn2[0:1, :] * kmax_ref[0:1, :]) * BOUND_SLACK

    ones_col = (lax.broadcasted_iota(i32, (kc, HEAD_DIM), 1) == 0).astype(bf16)
    for g in range(N_KV_HEADS):
        for r in range(n_rep):
            h = g * n_rep + r
            qcat_ref[g, r * tq:(r + 1) * tq, :] = q_ref[:, h * HEAD_DIM:(h + 1) * HEAD_DIM]

    def mask_t(c):
        r0 = pl.multiple_of(c * kc, kc)
        blk = key_ref[pl.ds(r0, kc), :]
        kpos = r0 + lax.broadcasted_iota(i32, (kc, 1), 0)
        sel = ((blk > thr) | ((blk == thr) & (kpos <= idx_cut))) & (blk > neg_inf_key)
        return jnp.where(sel, -m_fix, NEG_BIAS).T

    def group_logits(c, g, bias_t):
        r0 = pl.multiple_of(c * kc, kc)
        k_c = k_ref[pl.ds(r0, kc), g * HEAD_DIM:(g + 1) * HEAD_DIM]
        s = lax.dot_general(qcat_ref[g], k_c, _NT, preferred_element_type=f32)
        return (s.reshape(n_rep, tq, kc) + bias_t[None]).reshape(n_rep * tq, kc)

    def group_values(c, g):
        r0 = pl.multiple_of(c * kc, kc)
        return jnp.concatenate([v_ref[pl.ds(r0, kc), g * HEAD_DIM:(g + 1) * HEAD_DIM], ones_col],
                               axis=1)

    def write_group(g):
        acc = acc_ref[g]
        denom = acc[:, HEAD_DIM:HEAD_DIM + 1]
        o = acc[:, :HEAD_DIM] * (1.0 / denom)
        for r in range(n_rep):
            h = g * n_rep + r
            o_ref[:, h * HEAD_DIM:(h + 1) * HEAD_DIM] = o[r * tq:(r + 1) * tq].astype(o_ref.dtype)
        return jnp.min(denom)

    acc_ref[...] = jnp.zeros_like(acc_ref)

    def attend_chunk(c, carry):
        bias_t = mask_t(c)
        for g in range(N_KV_HEADS):
            p = jnp.exp2(group_logits(c, g, bias_t)).astype(bf16)
            acc_ref[g] += jnp.dot(p, group_values(c, g), preferred_element_type=f32)
        return carry

    lax.fori_loop(0, n_chunks, attend_chunk, 0)
    denom_min = write_group(0)
    for g in range(1, N_KV_HEADS):
        denom_min = jnp.minimum(denom_min, write_group(g))

    @pl.when(jnp.logical_not(denom_min >= DENOM_MIN))
    def _():
        acc_ref[...] = jnp.zeros_like(acc_ref)

        def attend_online(c, ms):
            bias_t = mask_t(c)
            new_ms = []
            for g in range(N_KV_HEADS):
                s = group_logits(c, g, bias_t)
                m_new = jnp.maximum(ms[g], s.max(axis=1, keepdims=True))
                p = jnp.exp2(s - m_new).astype(bf16)
                acc_ref[g] = (jnp.exp2(ms[g] - m_new) * acc_ref[g]
                              + jnp.dot(p, group_values(c, g), preferred_element_type=f32))
                new_ms.append(m_new)
            return tuple(new_ms)

        lax.fori_loop(0, n_chunks, attend_online,
                      tuple(jnp.full((n_rep * tq, 1), NEG_BIAS, f32) for _ in range(N_KV_HEADS)))
        for g in range(N_KV_HEADS):
            write_group(g)


def _dsa_attention(q, qi, wit, k, v, kia, kib, *, topk):
    S, q_cols = q.shape
    n_heads = q_cols // HEAD_DIM
    n_rep = n_heads // N_KV_HEADS
    kv_cols = k.shape[1]
    tq, kc = Q_TILE, KEY_CHUNK
    row = lambda i: (i, 0)
    const = lambda i: (0, 0)
    kern = functools.partial(_dsa_attention_kernel, topk=topk, n_heads=n_heads,
                             idx_bits=max(1, (S - 1).bit_length()))
    return pl.pallas_call(
        kern,
        grid=(S // tq,),
        in_specs=[
            pl.BlockSpec((tq, q_cols), row),
            pl.BlockSpec((tq, qi.shape[1]), row),
            pl.BlockSpec((IDX_HEADS, tq), lambda i: (0, i)),
            _resident((S, kv_cols), const),
            _resident((S, kv_cols), const),
            _resident((S, LANES), const),
            _resident((S, LANES), const),
        ],
        out_specs=pl.BlockSpec((tq, q_cols), row),
        out_shape=jax.ShapeDtypeStruct((S, q_cols), bf16),
        scratch_shapes=[pltpu.VMEM((S, tq), i32),
                        pltpu.VMEM((S, tq), i16),
                        pltpu.VMEM((S, tq), i16),
                        pltpu.VMEM((N_KV_HEADS, n_rep * tq, HEAD_DIM), bf16),
                        pltpu.VMEM((N_KV_HEADS, n_rep * tq, 2 * HEAD_DIM), f32),
                        pltpu.VMEM((8, tq), f32)],
        compiler_params=_params("arbitrary"),
        name="dsa_attention",
    )(q, qi, wit, k, v, kia, kib)


def _matmul_residual_kernel(a_ref, w_ref, res_ref, o_ref):
    a = a_ref[...]
    for c0 in range(0, o_ref.shape[1], COL_TILE):
        acc = jnp.dot(a, w_ref[:, c0:c0 + COL_TILE], preferred_element_type=f32)
        o_ref[:, c0:c0 + COL_TILE] = res_ref[:, c0:c0 + COL_TILE] + acc


def _matmul_residual(a, w, res):
    S, K = a.shape
    N = w.shape[1]
    tm = ROW_TILE
    row = lambda i: (i, 0)
    return pl.pallas_call(
        _matmul_residual_kernel,
        grid=(S // tm,),
        in_specs=[pl.BlockSpec((tm, K), row), _resident((K, N), lambda i: (0, 0)),
                  pl.BlockSpec((tm, N), row)],
        out_specs=pl.BlockSpec((tm, N), row),
        out_shape=jax.ShapeDtypeStruct((S, N), f32),
        compiler_params=_params("parallel"),
        name="matmul_residual",
    )(a, w, res)


def _conv_proj_kernel(x_ref, g_ref, w_ref, cw_ref, a_ref, tail_ref):
    tm, D = x_ref.shape

    @pl.when(pl.program_id(0) == 0)
    def _():
        tail_ref[...] = jnp.zeros_like(tail_ref)

    h = _rmsnorm_bf16(x_ref[...], g_ref[...])
    rowi = lax.broadcasted_iota(i32, (tm, COL_TILE), 0)
    for c0 in range(0, D, COL_TILE):
        cols = slice(c0, c0 + COL_TILE)
        b = jnp.dot(h, w_ref[:, c0:c0 + COL_TILE], preferred_element_type=f32)
        c = jnp.dot(h, w_ref[:, D + c0:D + c0 + COL_TILE], preferred_element_type=f32)
        u = jnp.dot(h, w_ref[:, 2 * D + c0:2 * D + c0 + COL_TILE], preferred_element_type=f32)
        z = c * u
        prev1 = tail_ref[7:8, cols]
        prev2 = tail_ref[6:7, cols]
        z1 = jnp.where(rowi == 0, prev1, pltpu.roll(z, 1, 0))
        z2 = jnp.where(rowi == 0, prev2, jnp.where(rowi == 1, prev1, pltpu.roll(z, 2, 0)))
        y = cw_ref[0:1, cols] * z2 + cw_ref[1:2, cols] * z1 + cw_ref[2:3, cols] * z
        a_ref[:, cols] = (b * y).astype(a_ref.dtype)
        tail_ref[:, cols] = z[tm - 8:tm, :]


def _conv_proj(x, g, w_in, conv_w):
    S, D = x.shape
    tm = ROW_TILE
    row = lambda i: (i, 0)
    const = lambda i: (0, 0)
    return pl.pallas_call(
        _conv_proj_kernel,
        grid=(S // tm,),
        in_specs=[pl.BlockSpec((tm, D), row), _resident((1, D), const),
                  _resident((D, 3 * D), const), _resident((CONV_WIDTH, D), const)],
        out_specs=pl.BlockSpec((tm, D), row),
        out_shape=jax.ShapeDtypeStruct((S, D), bf16),
        scratch_shapes=[pltpu.VMEM((8, D), f32)],
        compiler_params=_params("arbitrary"),
        name="conv_proj",
    )(x, g, w_in, conv_w)


def _ffn_kernel(x_ref, g_ref, wg_ref, wu_ref, wd_ref, o_ref, h_ref, acc_ref):
    j = pl.program_id(1)

    @pl.when(j == 0)
    def _():
        h_ref[...] = _rmsnorm_bf16(x_ref[...], g_ref[...])
        acc_ref[...] = jnp.zeros_like(acc_ref)

    h = h_ref[...]
    gate = jnp.dot(h, wg_ref[...], preferred_element_type=f32)
    up = jnp.dot(h, wu_ref[...], preferred_element_type=f32)
    act = (gate * jax.nn.sigmoid(gate) * up).astype(bf16)
    acc_ref[...] += jnp.dot(act, wd_ref[...], preferred_element_type=f32)

    @pl.when(j == pl.num_programs(1) - 1)
    def _():
        o_ref[...] = x_ref[...] + acc_ref[...]


def _ffn(x, g, wg, wu, wd):
    S, D = x.shape
    H = wg.shape[1]
    tm, th = ROW_TILE, COL_TILE
    return pl.pallas_call(
        _ffn_kernel,
        grid=(S // tm, H // th),
        in_specs=[pl.BlockSpec((tm, D), lambda i, j: (i, 0)),
                  _resident((1, D), lambda i, j: (0, 0)),
                  pl.BlockSpec((D, th), lambda i, j: (0, j)),
                  pl.BlockSpec((D, th), lambda i, j: (0, j)),
                  pl.BlockSpec((th, D), lambda i, j: (j, 0))],
        out_specs=pl.BlockSpec((tm, D), lambda i, j: (i, 0)),
        out_shape=jax.ShapeDtypeStruct((S, D), f32),
        scratch_shapes=[pltpu.VMEM((tm, D), bf16), pltpu.VMEM((tm, D), f32)],
        compiler_params=_params("parallel", "arbitrary"),
        name="ffn",
    )(x, g, wg, wu, wd)


def _rope_tables(seq_len):
    def angles(dim):
        inv_freq = 1.0 / (ROPE_THETA ** (jnp.arange(0, dim, 2, dtype=f32) / dim))
        return jnp.arange(seq_len, dtype=f32)[:, None] * inv_freq[None, :]

    ah, ai = angles(HEAD_DIM), angles(IDX_DIM)
    cosh = jnp.concatenate([jnp.cos(ah)] * 2, axis=-1)
    sinh = jnp.concatenate([-jnp.sin(ah), jnp.sin(ah)], axis=-1)
    cosi = jnp.concatenate([jnp.cos(ai)] * 4, axis=-1)
    sini = jnp.concatenate([-jnp.sin(ai), jnp.sin(ai)] * 2, axis=-1)
    return cosh, sinh, cosi, sini


def _attention_layer(x, norm_g, w_in, q_norm, k_norm, w_out, tables):
    S, D = x.shape
    q_cols = D
    kv_cols = N_KV_HEADS * HEAD_DIM
    qi_cols = IDX_HEADS * IDX_DIM
    pad = (-w_in.shape[1]) % LANES
    w_all = jnp.pad(w_in, ((0, 0), (0, pad))).astype(bf16)
    q, k, v, qi, kia, kib, wi = _attn_proj(
        x, norm_g[None, :], w_all, q_norm[None, :], k_norm[None, :], tables,
        q_cols=q_cols, kv_cols=kv_cols, qi_cols=qi_cols)
    wit = wi[:, IDX_DIM:IDX_DIM + IDX_HEADS].T
    o = _dsa_attention(q, qi, wit, k, v, kia, kib, topk=min(TOPK_MAX, S // 4))
    return _matmul_residual(o, w_out.astype(bf16), x)


def _conv_layer(x, norm_g, w_in, conv_w, w_out):
    a = _conv_proj(x, norm_g[None, :], w_in.astype(bf16), conv_w)
    return _matmul_residual(a, w_out.astype(bf16), x)


def kernel(x, attn_norm, attn_w_in, attn_q_norm, attn_k_norm, attn_w_out, conv_norm, conv_w_in,
           conv_w, conv_w_out, ffn_norm, ffn_w_gate, ffn_w_up, ffn_w_down):
    B, S, D = x.shape
    assert B == 1 and S % KEY_CHUNK == 0 and D % COL_TILE == 0
    depth = ffn_norm.shape[0]
    tables = _rope_tables(S)
    xs = x[0]
    for i in range(depth):
        j = i // 2
        if i % 2 == 0:
            xs = _attention_layer(xs, attn_norm[j], attn_w_in[j], attn_q_norm[j], attn_k_norm[j],
                                  attn_w_out[j], tables)
        else:
            xs = _conv_layer(xs, conv_norm[j], conv_w_in[j], conv_w[j], conv_w_out[j])
        xs = _ffn(xs, ffn_norm[i][None, :], ffn_w_gate[i].astype(bf16), ffn_w_up[i].astype(bf16),
                  ffn_w_down[i].astype(bf16))
    return xs[None]
```

```python
import functools

import jax
import jax.numpy as jnp
from jax import lax
from jax.experimental import pallas as pl
from jax.experimental.pallas import tpu as pltpu

HEAD_DIM = 128
N_KV_HEADS = 4
IDX_HEADS = 16
IDX_DIM = 64
TOPK_MAX = 256
CONV_WIDTH = 3
ROPE_THETA = 10000.0
RMS_EPS = 1e-6

LANES = 128
VMEM_LIMIT_BYTES = 56 * 1024 * 1024

ROW_TILE = 512
COL_TILE = 512
FFN_ROW_TILE = 1024
FFN_HIDDEN_TILE = 256
Q_TILE = 256
KEY_CHUNK = 512
COUNT_ROWS = 64

NEG_BIAS = -1e30
LOG2_E = 1.4426950408889634
BOUND_SLACK = 1.01
DENOM_MIN = 2.0 ** -100
I16_MIN, I16_MAX = -2 ** 15, 2 ** 15 - 1

_NT = (((1,), (1,)), ((), ()))

f32 = jnp.float32
bf16 = jnp.bfloat16
i32 = jnp.int32
i16 = jnp.int16


def _resident(block_shape, index_map):
    return pl.BlockSpec(block_shape, index_map, pipeline_mode=pl.Buffered(1))


def _layer_resident(stacked, layer):
    zeros = (0,) * (stacked.ndim - 1)
    return _resident((None,) + stacked.shape[1:], lambda *_: (layer,) + zeros)


def _params(*semantics):
    return pltpu.CompilerParams(dimension_semantics=semantics, vmem_limit_bytes=VMEM_LIMIT_BYTES)


def _rmsnorm_bf16(x, g):
    ms = jnp.mean(x * x, axis=-1, keepdims=True)
    return (x * lax.rsqrt(ms + RMS_EPS) * g).astype(bf16)


def _rope_pairs(y, cos, sin, half):
    if 2 * half == LANES:
        partner = pltpu.roll(y, half, 1)
    else:
        lane = lax.broadcasted_iota(i32, y.shape, 1)
        partner = jnp.where((lane & (2 * half - 1)) < half,
                            pltpu.roll(y, LANES - half, 1), pltpu.roll(y, half, 1))
    return y * cos + partner * sin


def _attn_proj_kernel(x_ref, g_ref, w_ref, qg_ref, kg_ref, cosh_ref, sinh_ref,
                      cosi_ref, sini_ref,
                      q_ref, k_ref, v_ref, qi_ref, kia_ref, kib_ref, wi_ref,
                      *, q_cols, kv_cols, qi_cols, q_scale, wi_scale):
    h = _rmsnorm_bf16(x_ref[...], g_ref[...])
    cosh, sinh = cosh_ref[...], sinh_ref[...]
    cosi, sini = cosi_ref[...], sini_ref[...]

    def head_norm_rope(acc, gain, scale, out_ref, col0):
        for t in range(acc.shape[1] // HEAD_DIM):
            y = acc[:, t * HEAD_DIM:(t + 1) * HEAD_DIM]
            ms = jnp.mean(y * y, axis=-1, keepdims=True)
            y = y * lax.rsqrt(ms + RMS_EPS) * gain
            y = _rope_pairs(y, cosh, sinh, HEAD_DIM // 2)
            if scale != 1.0:
                y = y * scale
            out_ref[:, col0 + t * HEAD_DIM:col0 + (t + 1) * HEAD_DIM] = y.astype(out_ref.dtype)

    col = 0
    for c0 in range(0, q_cols, COL_TILE):
        acc = jnp.dot(h, w_ref[:, col + c0:col + c0 + COL_TILE], preferred_element_type=f32)
        head_norm_rope(acc, qg_ref[...], q_scale, q_ref, c0)
    col += q_cols
    for c0 in range(0, kv_cols, COL_TILE):
        acc = jnp.dot(h, w_ref[:, col + c0:col + c0 + COL_TILE], preferred_element_type=f32)
        head_norm_rope(acc, kg_ref[...], 1.0, k_ref, c0)
    col += kv_cols
    for c0 in range(0, kv_cols, COL_TILE):
        acc = jnp.dot(h, w_ref[:, col + c0:col + c0 + COL_TILE], preferred_element_type=f32)
        v_ref[:, c0:c0 + COL_TILE] = acc.astype(v_ref.dtype)
    col += kv_cols
    for c0 in range(0, qi_cols, COL_TILE):
        acc = jnp.dot(h, w_ref[:, col + c0:col + c0 + COL_TILE], preferred_element_type=f32)
        for t in range(COL_TILE // LANES):
            y = _rope_pairs(acc[:, t * LANES:(t + 1) * LANES], cosi, sini, IDX_DIM // 2)
            qi_ref[:, c0 + t * LANES:c0 + (t + 1) * LANES] = y.astype(qi_ref.dtype)
    col += qi_cols
    acc = jnp.dot(h, w_ref[:, col:col + LANES], preferred_element_type=f32)
    ki = _rope_pairs(acc, cosi, sini, IDX_DIM // 2)
    lane = lax.broadcasted_iota(i32, acc.shape, 1)
    kia_ref[...] = jnp.where(lane < IDX_DIM, ki, 0.0).astype(kia_ref.dtype)
    kib_ref[...] = jnp.where(lane >= IDX_DIM, pltpu.roll(ki, IDX_DIM, 1), 0.0).astype(kib_ref.dtype)
    wi_ref[...] = acc * wi_scale


def _attn_proj(x, g, w_all, qg, kg, tables, layer, *, q_cols, kv_cols, qi_cols):
    S, D = x.shape
    tm = ROW_TILE
    cosh, sinh, cosi, sini = tables
    row = lambda i: (i, 0)
    kern = functools.partial(_attn_proj_kernel, q_cols=q_cols, kv_cols=kv_cols, qi_cols=qi_cols,
                             q_scale=HEAD_DIM ** -0.5 * LOG2_E,
                             wi_scale=IDX_HEADS ** -0.5 * IDX_DIM ** -0.5)
    return pl.pallas_call(
        kern,
        grid=(S // tm,),
        in_specs=[
            pl.BlockSpec((tm, D), row),
            _layer_resident(g, layer),
            _layer_resident(w_all, layer),
            _layer_resident(qg, layer),
            _layer_resident(kg, layer),
            pl.BlockSpec((tm, LANES), row), pl.BlockSpec((tm, LANES), row),
            pl.BlockSpec((tm, LANES), row), pl.BlockSpec((tm, LANES), row),
        ],
        out_specs=[
            pl.BlockSpec((tm, q_cols), row),
            pl.BlockSpec((tm, kv_cols), row),
            pl.BlockSpec((tm, kv_cols), row),
            pl.BlockSpec((tm, qi_cols), row),
            pl.BlockSpec((tm, LANES), row),
            pl.BlockSpec((tm, LANES), row),
            pl.BlockSpec((tm, LANES), row),
        ],
        out_shape=[
            jax.ShapeDtypeStruct((S, q_cols), bf16),
            jax.ShapeDtypeStruct((S, kv_cols), bf16),
            jax.ShapeDtypeStruct((S, kv_cols), bf16),
            jax.ShapeDtypeStruct((S, qi_cols), bf16),
            jax.ShapeDtypeStruct((S, LANES), bf16),
            jax.ShapeDtypeStruct((S, LANES), bf16),
            jax.ShapeDtypeStruct((S, LANES), f32),
        ],
        compiler_params=_params("parallel"),
        name="attn_proj",
    )(x, g, w_all, qg, kg, cosh, sinh, cosi, sini)


def _dsa_attention_kernel(q_ref, qi_ref, wit_ref, k_ref, v_ref, kia_ref, kib_ref, o_ref,
                          key_ref, hi_ref, lo_ref, qcat_ref, acc_ref, kmax_ref,
                          *, topk, n_heads, idx_bits):
    tq, kc = Q_TILE, KEY_CHUNK
    q0 = pl.program_id(0) * tq
    n_chunks = lax.div(q0 + tq + kc - 1, kc)
    qpos = q0 + lax.broadcasted_iota(i32, (1, tq), 1)
    n_rep = n_heads // N_KV_HEADS
    neg_inf_key = jnp.int32(-2139095041)

    def score_chunk(c, carry):
        r0 = pl.multiple_of(c * kc, kc)
        ka = kia_ref[pl.ds(r0, kc), :]
        kb = kib_ref[pl.ds(r0, kc), :]
        acc = jnp.zeros((kc, tq), f32)
        for p in range(IDX_HEADS // 2):
            rhs = qi_ref[:, p * LANES:(p + 1) * LANES]
            sa = lax.dot_general(ka, rhs, _NT, preferred_element_type=f32)
            sb = lax.dot_general(kb, rhs, _NT, preferred_element_type=f32)
            acc = acc + jnp.maximum(sa, 0.0) * wit_ref[2 * p:2 * p + 1, :]
            acc = acc + jnp.maximum(sb, 0.0) * wit_ref[2 * p + 1:2 * p + 2, :]
        kpos = r0 + lax.broadcasted_iota(i32, (kc, 1), 0)
        acc = jnp.where(kpos <= qpos, acc, -jnp.inf)
        bits = pltpu.bitcast(acc, i32)
        key = bits ^ ((bits >> 31) & jnp.int32(0x7FFFFFFF))
        key_ref[pl.ds(r0, kc), :] = key
        hi_ref[pl.ds(r0, kc), :] = (key >> 16).astype(i16)
        lo_ref[pl.ds(r0, kc), :] = ((key & 0xFFFF) + I16_MIN).astype(i16)
        return carry

    lax.fori_loop(0, n_chunks, score_chunk, 0)

    def count_where(pred):
        def body(c, cnt):
            r0 = pl.multiple_of(c * kc, kc)
            hit = jnp.where(pred(key_ref[pl.ds(r0, kc), :], r0), 1, 0).astype(i32)
            return cnt + hit.reshape(kc // COUNT_ROWS, COUNT_ROWS, tq).sum(axis=0)
        cnt = lax.fori_loop(0, n_chunks, body, jnp.zeros((COUNT_ROWS, tq), i32))
        return cnt.sum(axis=0, keepdims=True)

    def count_ge16(ref, cand):
        cand16 = cand.astype(i16)

        def body(c, cnt):
            r0 = pl.multiple_of(c * kc, kc)
            hit = jnp.where(ref[pl.ds(r0, kc), :] >= cand16, jnp.int16(1), jnp.int16(0))
            for t in range(kc // COUNT_ROWS):
                cnt = cnt + hit[t * COUNT_ROWS:(t + 1) * COUNT_ROWS]
            return cnt
        cnt = lax.fori_loop(0, n_chunks, body, jnp.zeros((COUNT_ROWS, tq), i16))
        return cnt.astype(i32).sum(axis=0, keepdims=True)

    def search16(ref, cnt_all):
        def bit(it, carry):
            t, cnt_ge = carry
            cand = t ^ jnp.left_shift(jnp.int32(1), 15 - it)
            cand = jnp.where(it == 0, jnp.zeros_like(cand), cand)
            tot = count_ge16(ref, cand)
            take = tot >= topk
            return jnp.where(take, cand, t), jnp.where(take, tot, cnt_ge)
        return lax.fori_loop(0, 16, bit, (jnp.full((1, tq), I16_MIN, i32), cnt_all))

    prefix, cnt_hi = search16(hi_ref, jnp.full((1, tq), n_chunks * kc, i32))
    prefix16 = prefix.astype(i16)

    def pin_chunk(c, carry):
        r0 = pl.multiple_of(c * kc, kc)
        hi = hi_ref[pl.ds(r0, kc), :]
        lo = lo_ref[pl.ds(r0, kc), :]
        lo_ref[pl.ds(r0, kc), :] = jnp.where(hi > prefix16, jnp.int16(I16_MAX),
                                             jnp.where(hi < prefix16, jnp.int16(I16_MIN), lo))
        return carry

    lax.fori_loop(0, n_chunks, pin_chunk, 0)
    low, cnt_ge = search16(lo_ref, cnt_hi)
    thr = prefix * 65536 + (low - I16_MIN)

    def no_ties():
        return jnp.full((1, tq), jnp.iinfo(jnp.int32).max, i32)

    def break_ties():
        n_eq = count_where(lambda blk, r0: blk == thr)
        want = topk - (cnt_ge - n_eq)

        def index_bit(it, cut):
            cand = cut | jnp.left_shift(jnp.int32(1), idx_bits - 1 - it)

            def pred(blk, r0):
                kpos = r0 + lax.broadcasted_iota(i32, (kc, 1), 0)
                return (blk == thr) & (kpos < cand)
            return jnp.where(count_where(pred) < want, cand, cut)

        return lax.fori_loop(0, idx_bits, index_bit, jnp.zeros((1, tq), i32))

    idx_cut = lax.cond(jnp.max(cnt_ge) > topk, break_ties, no_ties)

    @pl.when(pl.program_id(0) == 0)
    def _():
        def knorm_chunk(c, best):
            r0 = pl.multiple_of(c * kc, kc)
            kf = k_ref[pl.ds(r0, kc), :].astype(f32)
            sq = kf * kf
            for g in range(N_KV_HEADS):
                n2 = jnp.sum(sq[:, g * HEAD_DIM:(g + 1) * HEAD_DIM], axis=1, keepdims=True)
                best = jnp.maximum(best, jnp.max(n2, axis=0, keepdims=True))
            return best

        best = lax.fori_loop(0, k_ref.shape[0] // kc, knorm_chunk, jnp.zeros((1, 1), f32))
        kmax_ref[...] = jnp.broadcast_to(best, kmax_ref.shape)

    ones = jnp.ones((8, HEAD_DIM), f32)
    qn2 = jnp.zeros((8, tq), f32)
    for h in range(n_heads):
        qf = q_ref[:, h * HEAD_DIM:(h + 1) * HEAD_DIM].astype(f32)
        qn2 = jnp.maximum(qn2, lax.dot_general(ones, qf * qf, _NT, preferred_element_type=f32))
    m_fix = jnp.sqrt(qn2[0:1, :] * kmax_ref[0:1, :]) * BOUND_SLACK

    ones_col = (lax.broadcasted_iota(i32, (kc, HEAD_DIM), 1) == 0).astype(bf16)
    for g in range(N_KV_HEADS):
        for r in range(n_rep):
            h = g * n_rep + r
            qcat_ref[g, r * tq:(r + 1) * tq, :] = q_ref[:, h * HEAD_DIM:(h + 1) * HEAD_DIM]

    def mask_t(c):
        r0 = pl.multiple_of(c * kc, kc)
        blk = key_ref[pl.ds(r0, kc), :]
        kpos = r0 + lax.broadcasted_iota(i32, (kc, 1), 0)
        sel = ((blk > thr) | ((blk == thr) & (kpos <= idx_cut))) & (blk > neg_inf_key)
        return jnp.where(sel, -m_fix, NEG_BIAS).T

    def group_logits(c, g, bias_t):
        r0 = pl.multiple_of(c * kc, kc)
        k_c = k_ref[pl.ds(r0, kc), g * HEAD_DIM:(g + 1) * HEAD_DIM]
        s = lax.dot_general(qcat_ref[g], k_c, _NT, preferred_element_type=f32)
        return (s.reshape(n_rep, tq, kc) + bias_t[None]).reshape(n_rep * tq, kc)

    def group_values(c, g):
        r0 = pl.multiple_of(c * kc, kc)
        return jnp.concatenate([v_ref[pl.ds(r0, kc), g * HEAD_DIM:(g + 1) * HEAD_DIM], ones_col],
                               axis=1)

    def write_group(g):
        acc = acc_ref[g]
        denom = acc[:, HEAD_DIM:HEAD_DIM + 1]
        o = acc[:, :HEAD_DIM] * (1.0 / denom)
        for r in range(n_rep):
            h = g * n_rep + r
            o_ref[:, h * HEAD_DIM:(h + 1) * HEAD_DIM] = o[r * tq:(r + 1) * tq].astype(o_ref.dtype)
        return jnp.min(denom)

    acc_ref[...] = jnp.zeros_like(acc_ref)

    def attend_chunk(c, carry):
        bias_t = mask_t(c)
        for g in range(N_KV_HEADS):
            p = jnp.exp2(group_logits(c, g, bias_t)).astype(bf16)
            acc_ref[g] += jnp.dot(p, group_values(c, g), preferred_element_type=f32)
        return carry

    lax.fori_loop(0, n_chunks, attend_chunk, 0)
    denom_min = write_group(0)
    for g in range(1, N_KV_HEADS):
        denom_min = jnp.minimum(denom_min, write_group(g))

    @pl.when(jnp.logical_not(denom_min >= DENOM_MIN))
    def _():
        acc_ref[...] = jnp.zeros_like(acc_ref)

        def attend_online(c, ms):
            bias_t = mask_t(c)
            new_ms = []
            for g in range(N_KV_HEADS):
                s = group_logits(c, g, bias_t)
                m_new = jnp.maximum(ms[g], s.max(axis=1, keepdims=True))
                p = jnp.exp2(s - m_new).astype(bf16)
                acc_ref[g] = (jnp.exp2(ms[g] - m_new) * acc_ref[g]
                              + jnp.dot(p, group_values(c, g), preferred_element_type=f32))
                new_ms.append(m_new)
            return tuple(new_ms)

        lax.fori_loop(0, n_chunks, attend_online,
                      tuple(jnp.full((n_rep * tq, 1), NEG_BIAS, f32) for _ in range(N_KV_HEADS)))
        for g in range(N_KV_HEADS):
            write_group(g)


def _dsa_attention(q, qi, wit, k, v, kia, kib, *, topk):
    S, q_cols = q.shape
    n_heads = q_cols // HEAD_DIM
    n_rep = n_heads // N_KV_HEADS
    kv_cols = k.shape[1]
    tq, kc = Q_TILE, KEY_CHUNK
    row = lambda i: (i, 0)
    const = lambda i: (0, 0)
    kern = functools.partial(_dsa_attention_kernel, topk=topk, n_heads=n_heads,
                             idx_bits=max(1, (S - 1).bit_length()))
    return pl.pallas_call(
        kern,
        grid=(S // tq,),
        in_specs=[
            pl.BlockSpec((tq, q_cols), row),
            pl.BlockSpec((tq, qi.shape[1]), row),
            pl.BlockSpec((IDX_HEADS, tq), lambda i: (0, i)),
            _resident((S, kv_cols), const),
            _resident((S, kv_cols), const),
            _resident((S, LANES), const),
            _resident((S, LANES), const),
        ],
        out_specs=pl.BlockSpec((tq, q_cols), row),
        out_shape=jax.ShapeDtypeStruct((S, q_cols), bf16),
        scratch_shapes=[pltpu.VMEM((S, tq), i32),
                        pltpu.VMEM((S, tq), i16),
                        pltpu.VMEM((S, tq), i16),
                        pltpu.VMEM((N_KV_HEADS, n_rep * tq, HEAD_DIM), bf16),
                        pltpu.VMEM((N_KV_HEADS, n_rep * tq, 2 * HEAD_DIM), f32),
                        pltpu.VMEM((8, tq), f32)],
        compiler_params=_params("arbitrary"),
        name="dsa_attention",
    )(q, qi, wit, k, v, kia, kib)


def _matmul_residual_kernel(a_ref, w_ref, res_ref, o_ref):
    a = a_ref[...]
    for c0 in range(0, o_ref.shape[1], COL_TILE):
        acc = jnp.dot(a, w_ref[:, c0:c0 + COL_TILE], preferred_element_type=f32)
        o_ref[:, c0:c0 + COL_TILE] = res_ref[:, c0:c0 + COL_TILE] + acc


def _matmul_residual(a, w, res, layer):
    S, K = a.shape
    N = w.shape[2]
    tm = ROW_TILE
    row = lambda i: (i, 0)
    return pl.pallas_call(
        _matmul_residual_kernel,
        grid=(S // tm,),
        in_specs=[pl.BlockSpec((tm, K), row), _layer_resident(w, layer),
                  pl.BlockSpec((tm, N), row)],
        out_specs=pl.BlockSpec((tm, N), row),
        out_shape=jax.ShapeDtypeStruct((S, N), f32),
        compiler_params=_params("parallel"),
        name="matmul_residual",
    )(a, w, res)


def _conv_proj_kernel(x_ref, g_ref, w_ref, cw_ref, a_ref, tail_ref):
    tm, D = x_ref.shape

    @pl.when(pl.program_id(0) == 0)
    def _():
        tail_ref[...] = jnp.zeros_like(tail_ref)

    h = _rmsnorm_bf16(x_ref[...], g_ref[...])
    rowi = lax.broadcasted_iota(i32, (tm, COL_TILE), 0)
    for c0 in range(0, D, COL_TILE):
        cols = slice(c0, c0 + COL_TILE)
        b = jnp.dot(h, w_ref[:, c0:c0 + COL_TILE], preferred_element_type=f32)
        c = jnp.dot(h, w_ref[:, D + c0:D + c0 + COL_TILE], preferred_element_type=f32)
        u = jnp.dot(h, w_ref[:, 2 * D + c0:2 * D + c0 + COL_TILE], preferred_element_type=f32)
        z = c * u
        prev1 = tail_ref[7:8, cols]
        prev2 = tail_ref[6:7, cols]
        z1 = jnp.where(rowi == 0, prev1, pltpu.roll(z, 1, 0))
        z2 = jnp.where(rowi == 0, prev2, jnp.where(rowi == 1, prev1, pltpu.roll(z, 2, 0)))
        y = cw_ref[0:1, cols] * z2 + cw_ref[1:2, cols] * z1 + cw_ref[2:3, cols] * z
        a_ref[:, cols] = (b * y).astype(a_ref.dtype)
        tail_ref[:, cols] = z[tm - 8:tm, :]


def _conv_proj(x, g, w_in, conv_w, layer):
    S, D = x.shape
    tm = ROW_TILE
    row = lambda i: (i, 0)
    return pl.pallas_call(
        _conv_proj_kernel,
        grid=(S // tm,),
        in_specs=[pl.BlockSpec((tm, D), row), _layer_resident(g, layer),
                  _layer_resident(w_in, layer), _layer_resident(conv_w, layer)],
        out_specs=pl.BlockSpec((tm, D), row),
        out_shape=jax.ShapeDtypeStruct((S, D), bf16),
        scratch_shapes=[pltpu.VMEM((8, D), f32)],
        compiler_params=_params("arbitrary"),
        name="conv_proj",
    )(x, g, w_in, conv_w)


def _ffn_kernel(x_ref, g_ref, wg_ref, wu_ref, wd_ref, o_ref, h_ref):
    @pl.when(pl.program_id(1) == 0)
    def _():
        x = x_ref[...]
        h_ref[...] = _rmsnorm_bf16(x, g_ref[...])
        o_ref[...] = x

    h = h_ref[...]
    gate = jnp.dot(h, wg_ref[...].astype(bf16), preferred_element_type=f32)
    up = jnp.dot(h, wu_ref[...].astype(bf16), preferred_element_type=f32)
    act = (gate * jax.nn.sigmoid(gate) * up).astype(bf16)
    wd = wd_ref[...].astype(bf16)
    for c0 in range(0, o_ref.shape[1], COL_TILE):
        o_ref[:, c0:c0 + COL_TILE] += jnp.dot(act, wd[:, c0:c0 + COL_TILE],
                                              preferred_element_type=f32)


def _ffn(x, g, wg, wu, wd, layer):
    S, D = x.shape
    H = wg.shape[2]
    tm, th = FFN_ROW_TILE, FFN_HIDDEN_TILE
    return pl.pallas_call(
        _ffn_kernel,
        grid=(S // tm, H // th),
        in_specs=[pl.BlockSpec((tm, D), lambda i, j: (i, 0), pipeline_mode=pl.Buffered(1)),
                  _layer_resident(g, layer),
                  pl.BlockSpec((None, D, th), lambda i, j: (layer, 0, j)),
                  pl.BlockSpec((None, D, th), lambda i, j: (layer, 0, j)),
                  pl.BlockSpec((None, th, D), lambda i, j: (layer, j, 0))],
        out_specs=pl.BlockSpec((tm, D), lambda i, j: (i, 0)),
        out_shape=jax.ShapeDtypeStruct((S, D), f32),
        scratch_shapes=[pltpu.VMEM((tm, D), bf16)],
        compiler_params=_params("parallel", "arbitrary"),
        name="ffn",
    )(x, g, wg, wu, wd)


def _rope_tables(seq_len):
    def angles(dim):
        inv_freq = 1.0 / (ROPE_THETA ** (jnp.arange(0, dim, 2, dtype=f32) / dim))
        return jnp.arange(seq_len, dtype=f32)[:, None] * inv_freq[None, :]

    ah, ai = angles(HEAD_DIM), angles(IDX_DIM)
    cosh = jnp.concatenate([jnp.cos(ah)] * 2, axis=-1)
    sinh = jnp.concatenate([-jnp.sin(ah), jnp.sin(ah)], axis=-1)
    cosi = jnp.concatenate([jnp.cos(ai)] * 4, axis=-1)
    sini = jnp.concatenate([-jnp.sin(ai), jnp.sin(ai)] * 2, axis=-1)
    return cosh, sinh, cosi, sini


def kernel(x, attn_norm, attn_w_in, attn_q_norm, attn_k_norm, attn_w_out, conv_norm, conv_w_in,
           conv_w, conv_w_out, ffn_norm, ffn_w_gate, ffn_w_up, ffn_w_down):
    B, S, D = x.shape
    assert B == 1 and S % max(KEY_CHUNK, FFN_ROW_TILE) == 0 and D % COL_TILE == 0
    assert ffn_w_gate.shape[2] % FFN_HIDDEN_TILE == 0
    depth = ffn_norm.shape[0]
    tables = _rope_tables(S)
    q_cols, kv_cols, qi_cols = D, N_KV_HEADS * HEAD_DIM, IDX_HEADS * IDX_DIM

    row_vec = lambda p: p[:, None, :]
    pad = (-attn_w_in.shape[2]) % LANES
    attn_w_all = jnp.pad(attn_w_in, ((0, 0), (0, 0), (0, pad))).astype(bf16)
    attn_w_out, conv_w_in, conv_w_out = (w.astype(bf16) for w in (attn_w_out, conv_w_in, conv_w_out))
    attn_norm, attn_q_norm, attn_k_norm, conv_norm, ffn_norm = (
        row_vec(p) for p in (attn_norm, attn_q_norm, attn_k_norm, conv_norm, ffn_norm))

    xs = x[0]
    for i in range(depth):
        j = i // 2
        if i % 2 == 0:
            q, k, v, qi, kia, kib, wi = _attn_proj(
                xs, attn_norm, attn_w_all, attn_q_norm, attn_k_norm, tables, j,
                q_cols=q_cols, kv_cols=kv_cols, qi_cols=qi_cols)
            wit = wi[:, IDX_DIM:IDX_DIM + IDX_HEADS].T
            o = _dsa_attention(q, qi, wit, k, v, kia, kib, topk=min(TOPK_MAX, S // 4))
            xs = _matmul_residual(o, attn_w_out, xs, j)
        else:
            a = _conv_proj(xs, conv_norm, conv_w_in, conv_w, j)
            xs = _matmul_residual(a, conv_w_out, xs, j)
        xs = _ffn(xs, ffn_norm, ffn_w_gate, ffn_w_up, ffn_w_down, i)
    return xs[None]
```

```python
import functools

import jax
import jax.numpy as jnp
import numpy as np
from jax import lax
from jax.experimental import pallas as pl
from jax.experimental.pallas import tpu as pltpu

HEAD_DIM = 128
N_KV_HEADS = 4
IDX_HEADS = 16
IDX_DIM = 64
TOPK_MAX = 256
CONV_WIDTH = 3
ROPE_THETA = 10000.0
RMS_EPS = 1e-6

LANES = 128
VMEM_LIMIT_BYTES = 56 * 1024 * 1024

ROW_TILE = 512
COL_TILE = 512
FFN_ROW_TILE = 1024
FFN_HIDDEN_TILE = 256
Q_TILE = 256
KEY_CHUNK = 512
COUNT_ROWS = 64

NEG_BIAS = -1e30
LOG2_E = 1.4426950408889634
BOUND_SLACK = 1.01
DENOM_MIN = 2.0 ** -100
I16_MIN, I16_MAX = -2 ** 15, 2 ** 15 - 1

_NT = (((1,), (1,)), ((), ()))

f32 = jnp.float32
bf16 = jnp.bfloat16
i32 = jnp.int32
i16 = jnp.int16


def _resident(block_shape, index_map):
    return pl.BlockSpec(block_shape, index_map, pipeline_mode=pl.Buffered(1))


def _layer_resident(stacked, layer):
    zeros = (0,) * (stacked.ndim - 1)
    return _resident((None,) + stacked.shape[1:], lambda *_: (layer,) + zeros)


def _params(*semantics):
    return pltpu.CompilerParams(dimension_semantics=semantics, vmem_limit_bytes=VMEM_LIMIT_BYTES)


def _rmsnorm_bf16(x, g):
    ms = jnp.mean(x * x, axis=-1, keepdims=True)
    return (x * lax.rsqrt(ms + RMS_EPS) * g).astype(bf16)


def _rope_pairs(y, cos, sin, half):
    if 2 * half == LANES:
        partner = pltpu.roll(y, half, 1)
    else:
        lane = lax.broadcasted_iota(i32, y.shape, 1)
        partner = jnp.where((lane & (2 * half - 1)) < half,
                            pltpu.roll(y, LANES - half, 1), pltpu.roll(y, half, 1))
    return y * cos + partner * sin


def _attn_proj_kernel(x_ref, g_ref, w_ref, qg_ref, kg_ref, cosh_ref, sinh_ref,
                      cosi_ref, sini_ref,
                      q_ref, k_ref, v_ref, qi_ref, ki2_ref, wi_ref,
                      *, q_cols, kv_cols, qi_cols, q_scale, wi_scale):
    h = _rmsnorm_bf16(x_ref[...], g_ref[...])
    cosh, sinh = cosh_ref[...], sinh_ref[...]
    cosi, sini = cosi_ref[...], sini_ref[...]

    def head_norm_rope(acc, gain, scale, out_ref, col0):
        for t in range(acc.shape[1] // HEAD_DIM):
            y = acc[:, t * HEAD_DIM:(t + 1) * HEAD_DIM]
            ms = jnp.mean(y * y, axis=-1, keepdims=True)
            y = y * lax.rsqrt(ms + RMS_EPS) * gain
            y = _rope_pairs(y, cosh, sinh, HEAD_DIM // 2)
            if scale != 1.0:
                y = y * scale
            out_ref[:, col0 + t * HEAD_DIM:col0 + (t + 1) * HEAD_DIM] = y.astype(out_ref.dtype)

    col = 0
    for c0 in range(0, q_cols, COL_TILE):
        acc = jnp.dot(h, w_ref[:, col + c0:col + c0 + COL_TILE], preferred_element_type=f32)
        head_norm_rope(acc, qg_ref[...], q_scale, q_ref, c0)
    col += q_cols
    for c0 in range(0, kv_cols, COL_TILE):
        acc = jnp.dot(h, w_ref[:, col + c0:col + c0 + COL_TILE], preferred_element_type=f32)
        head_norm_rope(acc, kg_ref[...], 1.0, k_ref, c0)
    col += kv_cols
    for c0 in range(0, kv_cols, COL_TILE):
        acc = jnp.dot(h, w_ref[:, col + c0:col + c0 + COL_TILE], preferred_element_type=f32)
        v_ref[:, c0:c0 + COL_TILE] = acc.astype(v_ref.dtype)
    col += kv_cols
    for c0 in range(0, qi_cols, COL_TILE):
        acc = jnp.dot(h, w_ref[:, col + c0:col + c0 + COL_TILE], preferred_element_type=f32)
        for t in range(COL_TILE // LANES):
            y = _rope_pairs(acc[:, t * LANES:(t + 1) * LANES], cosi, sini, IDX_DIM // 2)
            qi_ref[:, c0 + t * LANES:c0 + (t + 1) * LANES] = y.astype(qi_ref.dtype)
    col += qi_cols
    acc = jnp.dot(h, w_ref[:, col:col + LANES], preferred_element_type=f32)
    ki = _rope_pairs(acc, cosi, sini, IDX_DIM // 2)
    lane = lax.broadcasted_iota(i32, acc.shape, 1)
    ki2_ref[...] = jnp.where(lane < IDX_DIM, ki, pltpu.roll(ki, IDX_DIM, 1)).astype(ki2_ref.dtype)
    wi_ref[...] = acc * wi_scale


def _attn_proj(x, g, w_all, qg, kg, tables, layer, *, q_cols, kv_cols, qi_cols):
    S, D = x.shape
    tm = ROW_TILE
    cosh, sinh, cosi, sini = tables
    row = lambda i: (i, 0)
    kern = functools.partial(_attn_proj_kernel, q_cols=q_cols, kv_cols=kv_cols, qi_cols=qi_cols,
                             q_scale=HEAD_DIM ** -0.5 * LOG2_E,
                             wi_scale=IDX_HEADS ** -0.5 * IDX_DIM ** -0.5)
    return pl.pallas_call(
        kern,
        grid=(S // tm,),
        in_specs=[
            pl.BlockSpec((tm, D), row),
            _layer_resident(g, layer),
            _layer_resident(w_all, layer),
            _layer_resident(qg, layer),
            _layer_resident(kg, layer),
            pl.BlockSpec((tm, LANES), row), pl.BlockSpec((tm, LANES), row),
            pl.BlockSpec((tm, LANES), row), pl.BlockSpec((tm, LANES), row),
        ],
        out_specs=[
            pl.BlockSpec((tm, q_cols), row),
            pl.BlockSpec((tm, kv_cols), row),
            pl.BlockSpec((tm, kv_cols), row),
            pl.BlockSpec((tm, qi_cols), row),
            pl.BlockSpec((tm, LANES), row),
            pl.BlockSpec((tm, LANES), row),
        ],
        out_shape=[
            jax.ShapeDtypeStruct((S, q_cols), bf16),
            jax.ShapeDtypeStruct((S, kv_cols), bf16),
            jax.ShapeDtypeStruct((S, kv_cols), bf16),
            jax.ShapeDtypeStruct((S, qi_cols), bf16),
            jax.ShapeDtypeStruct((S, LANES), bf16),
            jax.ShapeDtypeStruct((S, LANES), f32),
        ],
        compiler_params=_params("parallel"),
        name="attn_proj",
    )(x, g, w_all, qg, kg, cosh, sinh, cosi, sini)


def _dsa_attention_kernel(q_ref, qi_ref, wit_ref, k_ref, v_ref, ki2_ref, o_ref,
                          key_ref, hi_ref, lo_ref, qcat_ref, p_ref, acc_ref, kmax_ref,
                          *, topk, n_heads, idx_bits):
    tq, kc = Q_TILE, KEY_CHUNK
    q0 = pl.program_id(0) * tq
    n_chunks = lax.div(q0 + tq + kc - 1, kc)
    qpos = q0 + lax.broadcasted_iota(i32, (1, tq), 1)
    n_rep = n_heads // N_KV_HEADS
    neg_inf_key = jnp.int32(-2139095041)

    def score_chunk(c, carry):
        r0 = pl.multiple_of(c * kc, kc)
        ki2 = ki2_ref[pl.ds(r0, kc), :]
        first_head = lax.broadcasted_iota(i32, ki2.shape, 1) < IDX_DIM
        ka = jnp.where(first_head, ki2, jnp.zeros_like(ki2))
        kb = jnp.where(first_head, jnp.zeros_like(ki2), ki2)
        acc = jnp.zeros((kc, tq), f32)
        for p in range(IDX_HEADS // 2):
            rhs = qi_ref[:, p * LANES:(p + 1) * LANES]
            sa = lax.dot_general(ka, rhs, _NT, preferred_element_type=f32)
            sb = lax.dot_general(kb, rhs, _NT, preferred_element_type=f32)
            acc = acc + jnp.maximum(sa, 0.0) * wit_ref[2 * p:2 * p + 1, :]
            acc = acc + jnp.maximum(sb, 0.0) * wit_ref[2 * p + 1:2 * p + 2, :]
        kpos = r0 + lax.broadcasted_iota(i32, (kc, 1), 0)
        acc = jnp.where(kpos <= qpos, acc, -jnp.inf)
        bits = pltpu.bitcast(acc, i32)
        key = bits ^ ((bits >> 31) & jnp.int32(0x7FFFFFFF))
        key_ref[pl.ds(r0, kc), :] = key
        hi_ref[pl.ds(r0, kc), :] = (key >> 16).astype(i16)
        lo_ref[pl.ds(r0, kc), :] = ((key & 0xFFFF) + I16_MIN).astype(i16)
        return carry

    lax.fori_loop(0, n_chunks, score_chunk, 0)

    def count_where(pred):
        def body(c, cnt):
            r0 = pl.multiple_of(c * kc, kc)
            hit = jnp.where(pred(key_ref[pl.ds(r0, kc), :], r0), 1, 0).astype(i32)
            return cnt + hit.reshape(kc // COUNT_ROWS, COUNT_ROWS, tq).sum(axis=0)
        cnt = lax.fori_loop(0, n_chunks, body, jnp.zeros((COUNT_ROWS, tq), i32))
        return cnt.sum(axis=0, keepdims=True)

    def count_ge16(ref, cand):
        cand16 = cand.astype(i16)

        def body(c, cnt):
            r0 = pl.multiple_of(c * kc, kc)
            hit = jnp.where(ref[pl.ds(r0, kc), :] >= cand16, jnp.int16(1), jnp.int16(0))
            for t in range(kc // COUNT_ROWS):
                cnt = cnt + hit[t * COUNT_ROWS:(t + 1) * COUNT_ROWS]
            return cnt
        cnt = lax.fori_loop(0, n_chunks, body, jnp.zeros((COUNT_ROWS, tq), i16))
        return cnt.astype(i32).sum(axis=0, keepdims=True)

    def search16(ref, cnt_all):
        def bit(it, carry):
            t, cnt_ge = carry
            cand = t ^ jnp.left_shift(jnp.int32(1), 15 - it)
            cand = jnp.where(it == 0, jnp.zeros_like(cand), cand)
            tot = count_ge16(ref, cand)
            take = tot >= topk
            return jnp.where(take, cand, t), jnp.where(take, tot, cnt_ge)
        return lax.fori_loop(0, 16, bit, (jnp.full((1, tq), I16_MIN, i32), cnt_all))

    prefix, cnt_hi = search16(hi_ref, jnp.full((1, tq), n_chunks * kc, i32))
    prefix16 = prefix.astype(i16)

    def pin_chunk(c, carry):
        r0 = pl.multiple_of(c * kc, kc)
        hi = hi_ref[pl.ds(r0, kc), :]
        lo = lo_ref[pl.ds(r0, kc), :]
        lo_ref[pl.ds(r0, kc), :] = jnp.where(hi > prefix16, jnp.int16(I16_MAX),
                                             jnp.where(hi < prefix16, jnp.int16(I16_MIN), lo))
        return carry

    lax.fori_loop(0, n_chunks, pin_chunk, 0)
    low, cnt_ge = search16(lo_ref, cnt_hi)
    thr = prefix * 65536 + (low - I16_MIN)

    def no_ties():
        return jnp.full((1, tq), jnp.iinfo(jnp.int32).max, i32)

    def break_ties():
        n_eq = count_where(lambda blk, r0: blk == thr)
        want = topk - (cnt_ge - n_eq)

        def index_bit(it, cut):
            cand = cut | jnp.left_shift(jnp.int32(1), idx_bits - 1 - it)

            def pred(blk, r0):
                kpos = r0 + lax.broadcasted_iota(i32, (kc, 1), 0)
                return (blk == thr) & (kpos < cand)
            return jnp.where(count_where(pred) < want, cand, cut)

        return lax.fori_loop(0, idx_bits, index_bit, jnp.zeros((1, tq), i32))

    idx_cut = lax.cond(jnp.max(cnt_ge) > topk, break_ties, no_ties)

    @pl.when(pl.program_id(0) == 0)
    def _():
        def knorm_chunk(c, best):
            r0 = pl.multiple_of(c * kc, kc)
            kf = k_ref[pl.ds(r0, kc), :].astype(f32)
            sq = kf * kf
            for g in range(N_KV_HEADS):
                n2 = jnp.sum(sq[:, g * HEAD_DIM:(g + 1) * HEAD_DIM], axis=1, keepdims=True)
                best = jnp.maximum(best, jnp.max(n2, axis=0, keepdims=True))
            return best

        best = lax.fori_loop(0, k_ref.shape[0] // kc, knorm_chunk, jnp.zeros((1, 1), f32))
        kmax_ref[...] = jnp.broadcast_to(best, kmax_ref.shape)

    ones = jnp.ones((8, HEAD_DIM), f32)
    qn2 = jnp.zeros((8, tq), f32)
    for h in range(n_heads):
        qf = q_ref[:, h * HEAD_DIM:(h + 1) * HEAD_DIM].astype(f32)
        qn2 = jnp.maximum(qn2, lax.dot_general(ones, qf * qf, _NT, preferred_element_type=f32))
    m_fix = jnp.sqrt(qn2[0:1, :] * kmax_ref[0:1, :]) * BOUND_SLACK

    ones_col = (lax.broadcasted_iota(i32, (kc, HEAD_DIM), 1) == 0).astype(bf16)
    for g in range(N_KV_HEADS):
        for r in range(n_rep):
            h = g * n_rep + r
            qcat_ref[g, r * tq:(r + 1) * tq, :] = q_ref[:, h * HEAD_DIM:(h + 1) * HEAD_DIM]

    def mask_t(c):
        r0 = pl.multiple_of(c * kc, kc)
        blk = key_ref[pl.ds(r0, kc), :]
        kpos = r0 + lax.broadcasted_iota(i32, (kc, 1), 0)
        sel = ((blk > thr) | ((blk == thr) & (kpos <= idx_cut))) & (blk > neg_inf_key)
        return jnp.where(sel, -m_fix, NEG_BIAS).T

    def group_logits(c, g, bias_t):
        r0 = pl.multiple_of(c * kc, kc)
        k_c = k_ref[pl.ds(r0, kc), g * HEAD_DIM:(g + 1) * HEAD_DIM]
        s = lax.dot_general(qcat_ref[g], k_c, _NT, preferred_element_type=f32)
        return (s.reshape(n_rep, tq, kc) + bias_t[None]).reshape(n_rep * tq, kc)

    def group_values(c, g):
        r0 = pl.multiple_of(c * kc, kc)
        return jnp.concatenate([v_ref[pl.ds(r0, kc), g * HEAD_DIM:(g + 1) * HEAD_DIM], ones_col],
                               axis=1)

    def write_group(g):
        acc = acc_ref[g]
        denom = acc[:, HEAD_DIM:HEAD_DIM + 1]
        o = acc[:, :HEAD_DIM] * (1.0 / denom)
        for r in range(n_rep):
            h = g * n_rep + r
            o_ref[:, h * HEAD_DIM:(h + 1) * HEAD_DIM] = o[r * tq:(r + 1) * tq].astype(o_ref.dtype)
        return jnp.min(denom)

    acc_ref[...] = jnp.zeros_like(acc_ref)

    def probabilities(c):
        bias_t = mask_t(c)
        for g in range(N_KV_HEADS):
            p_ref[g] = jnp.exp2(group_logits(c, g, bias_t)).astype(bf16)

    def weighted_values(c):
        for g in range(N_KV_HEADS):
            acc_ref[g] += jnp.dot(p_ref[g], group_values(c, g), preferred_element_type=f32)

    def attend_chunk(c, carry):
        weighted_values(c - 1)
        probabilities(c)
        return carry

    probabilities(0)
    lax.fori_loop(1, n_chunks, attend_chunk, 0)
    weighted_values(n_chunks - 1)
    denom_min = write_group(0)
    for g in range(1, N_KV_HEADS):
        denom_min = jnp.minimum(denom_min, write_group(g))

    @pl.when(jnp.logical_not(denom_min >= DENOM_MIN))
    def _():
        acc_ref[...] = jnp.zeros_like(acc_ref)

        def attend_online(c, ms):
            bias_t = mask_t(c)
            new_ms = []
            for g in range(N_KV_HEADS):
                s = group_logits(c, g, bias_t)
                m_new = jnp.maximum(ms[g], s.max(axis=1, keepdims=True))
                p = jnp.exp2(s - m_new).astype(bf16)
                acc_ref[g] = (jnp.exp2(ms[g] - m_new) * acc_ref[g]
                              + jnp.dot(p, group_values(c, g), preferred_element_type=f32))
                new_ms.append(m_new)
            return tuple(new_ms)

        lax.fori_loop(0, n_chunks, attend_online,
                      tuple(jnp.full((n_rep * tq, 1), NEG_BIAS, f32) for _ in range(N_KV_HEADS)))
        for g in range(N_KV_HEADS):
            write_group(g)


def _dsa_attention(q, qi, wit, k, v, ki2, *, topk):
    S, q_cols = q.shape
    n_heads = q_cols // HEAD_DIM
    n_rep = n_heads // N_KV_HEADS
    kv_cols = k.shape[1]
    tq, kc = Q_TILE, KEY_CHUNK
    row = lambda i: (i, 0)
    const = lambda i: (0, 0)
    kern = functools.partial(_dsa_attention_kernel, topk=topk, n_heads=n_heads,
                             idx_bits=max(1, (S - 1).bit_length()))
    return pl.pallas_call(
        kern,
        grid=(S // tq,),
        in_specs=[
            pl.BlockSpec((tq, q_cols), row),
            pl.BlockSpec((tq, qi.shape[1]), row),
            pl.BlockSpec((IDX_HEADS, tq), lambda i: (0, i)),
            _resident((S, kv_cols), const),
            _resident((S, kv_cols), const),
            _resident((S, LANES), const),
        ],
        out_specs=pl.BlockSpec((tq, q_cols), row),
        out_shape=jax.ShapeDtypeStruct((S, q_cols), bf16),
        scratch_shapes=[pltpu.VMEM((S, tq), i32),
                        pltpu.VMEM((S, tq), i16),
                        pltpu.VMEM((S, tq), i16),
                        pltpu.VMEM((N_KV_HEADS, n_rep * tq, HEAD_DIM), bf16),
                        pltpu.VMEM((N_KV_HEADS, n_rep * tq, kc), bf16),
                        pltpu.VMEM((N_KV_HEADS, n_rep * tq, 2 * HEAD_DIM), f32),
                        pltpu.VMEM((8, tq), f32)],
        compiler_params=_params("arbitrary"),
        name="dsa_attention",
    )(q, qi, wit, k, v, ki2)


def _matmul_residual_kernel(a_ref, w_ref, res_ref, o_ref):
    a = a_ref[...]
    for c0 in range(0, o_ref.shape[1], COL_TILE):
        acc = jnp.dot(a, w_ref[:, c0:c0 + COL_TILE], preferred_element_type=f32)
        o_ref[:, c0:c0 + COL_TILE] = res_ref[:, c0:c0 + COL_TILE] + acc


def _matmul_residual(a, w, res, layer):
    S, K = a.shape
    N = w.shape[2]
    tm = ROW_TILE
    row = lambda i: (i, 0)
    return pl.pallas_call(
        _matmul_residual_kernel,
        grid=(S // tm,),
        in_specs=[pl.BlockSpec((tm, K), row), _layer_resident(w, layer),
                  pl.BlockSpec((tm, N), row)],
        out_specs=pl.BlockSpec((tm, N), row),
        out_shape=jax.ShapeDtypeStruct((S, N), f32),
        compiler_params=_params("parallel"),
        name="matmul_residual",
    )(a, w, res)


def _conv_proj_kernel(x_ref, g_ref, w_ref, cw_ref, a_ref, tail_ref):
    tm, D = x_ref.shape

    @pl.when(pl.program_id(0) == 0)
    def _():
        tail_ref[...] = jnp.zeros_like(tail_ref)

    h = _rmsnorm_bf16(x_ref[...], g_ref[...])
    rowi = lax.broadcasted_iota(i32, (tm, COL_TILE), 0)
    for c0 in range(0, D, COL_TILE):
        cols = slice(c0, c0 + COL_TILE)
        b = jnp.dot(h, w_ref[:, c0:c0 + COL_TILE], preferred_element_type=f32)
        c = jnp.dot(h, w_ref[:, D + c0:D + c0 + COL_TILE], preferred_element_type=f32)
        u = jnp.dot(h, w_ref[:, 2 * D + c0:2 * D + c0 + COL_TILE], preferred_element_type=f32)
        z = c * u
        prev1 = tail_ref[7:8, cols]
        prev2 = tail_ref[6:7, cols]
        z1 = jnp.where(rowi == 0, prev1, pltpu.roll(z, 1, 0))
        z2 = jnp.where(rowi == 0, prev2, jnp.where(rowi == 1, prev1, pltpu.roll(z, 2, 0)))
        y = cw_ref[0:1, cols] * z2 + cw_ref[1:2, cols] * z1 + cw_ref[2:3, cols] * z
        a_ref[:, cols] = (b * y).astype(a_ref.dtype)
        tail_ref[:, cols] = z[tm - 8:tm, :]


def _conv_proj(x, g, w_in, conv_w, layer):
    S, D = x.shape
    tm = ROW_TILE
    row = lambda i: (i, 0)
    return pl.pallas_call(
        _conv_proj_kernel,
        grid=(S // tm,),
        in_specs=[pl.BlockSpec((tm, D), row), _layer_resident(g, layer),
                  _layer_resident(w_in, layer), _layer_resident(conv_w, layer)],
        out_specs=pl.BlockSpec((tm, D), row),
        out_shape=jax.ShapeDtypeStruct((S, D), bf16),
        scratch_shapes=[pltpu.VMEM((8, D), f32)],
        compiler_params=_params("arbitrary"),
        name="conv_proj",
    )(x, g, w_in, conv_w)


def _ffn_kernel(x_ref, g_ref, wg_ref, wu_ref, wd_ref, o_ref, h_ref):
    @pl.when(pl.program_id(1) == 0)
    def _():
        x = x_ref[...]
        h_ref[...] = _rmsnorm_bf16(x, g_ref[...])
        o_ref[...] = x

    h = h_ref[...]
    gate = jnp.dot(h, wg_ref[...].astype(bf16), preferred_element_type=f32)
    up = jnp.dot(h, wu_ref[...].astype(bf16), preferred_element_type=f32)
    act = (gate * jax.nn.sigmoid(gate) * up).astype(bf16)
    wd = wd_ref[...].astype(bf16)
    for c0 in range(0, o_ref.shape[1], COL_TILE):
        o_ref[:, c0:c0 + COL_TILE] += jnp.dot(act, wd[:, c0:c0 + COL_TILE],
                                              preferred_element_type=f32)


def _ffn(x, g, wg, wu, wd, layer):
    S, D = x.shape
    H = wg.shape[2]
    tm, th = FFN_ROW_TILE, FFN_HIDDEN_TILE
    return pl.pallas_call(
        _ffn_kernel,
        grid=(S // tm, H // th),
        in_specs=[pl.BlockSpec((tm, D), lambda i, j: (i, 0), pipeline_mode=pl.Buffered(1)),
                  _layer_resident(g, layer),
                  pl.BlockSpec((None, D, th), lambda i, j: (layer, 0, j)),
                  pl.BlockSpec((None, D, th), lambda i, j: (layer, 0, j)),
                  pl.BlockSpec((None, th, D), lambda i, j: (layer, j, 0))],
        out_specs=pl.BlockSpec((tm, D), lambda i, j: (i, 0)),
        out_shape=jax.ShapeDtypeStruct((S, D), f32),
        scratch_shapes=[pltpu.VMEM((tm, D), bf16)],
        compiler_params=_params("parallel", "arbitrary"),
        name="ffn",
    )(x, g, wg, wu, wd)


def _rope_tables(seq_len):
    def cos_sin(dim):
        inv_freq = (1.0 / (np.float32(ROPE_THETA) ** (np.arange(0, dim, 2, dtype=np.float32) / dim)))
        ang = np.arange(seq_len, dtype=np.float32)[:, None] * inv_freq.astype(np.float32)[None, :]
        return np.cos(ang).astype(np.float32), np.sin(ang).astype(np.float32)

    ch, sh = (jnp.asarray(t) for t in cos_sin(HEAD_DIM))
    ci, si = (jnp.asarray(t) for t in cos_sin(IDX_DIM))
    cosh = jnp.concatenate([ch, ch], axis=-1)
    sinh = jnp.concatenate([-sh, sh], axis=-1)
    cosi = jnp.concatenate([ci] * 4, axis=-1)
    sini = jnp.concatenate([-si, si] * 2, axis=-1)
    return cosh, sinh, cosi, sini


def kernel(x, attn_norm, attn_w_in, attn_q_norm, attn_k_norm, attn_w_out, conv_norm, conv_w_in,
           conv_w, conv_w_out, ffn_norm, ffn_w_gate, ffn_w_up, ffn_w_down):
    B, S, D = x.shape
    assert B == 1 and S % max(KEY_CHUNK, FFN_ROW_TILE) == 0 and D % COL_TILE == 0
    assert ffn_w_gate.shape[2] % FFN_HIDDEN_TILE == 0
    depth = ffn_norm.shape[0]
    tables = _rope_tables(S)
    q_cols, kv_cols, qi_cols = D, N_KV_HEADS * HEAD_DIM, IDX_HEADS * IDX_DIM

    row_vec = lambda p: p[:, None, :]
    pad = (-attn_w_in.shape[2]) % LANES
    attn_w_all = jnp.pad(attn_w_in, ((0, 0), (0, 0), (0, pad))).astype(bf16)
    attn_w_out, conv_w_in, conv_w_out = (w.astype(bf16) for w in (attn_w_out, conv_w_in, conv_w_out))
    attn_norm, attn_q_norm, attn_k_norm, conv_norm, ffn_norm = (
        row_vec(p) for p in (attn_norm, attn_q_norm, attn_k_norm, conv_norm, ffn_norm))

    xs = x[0]
    for i in range(depth):
        j = i // 2
        if i % 2 == 0:
            q, k, v, qi, ki2, wi = _attn_proj(
                xs, attn_norm, attn_w_all, attn_q_norm, attn_k_norm, tables, j,
                q_cols=q_cols, kv_cols=kv_cols, qi_cols=qi_cols)
            wit = wi[:, IDX_DIM:IDX_DIM + IDX_HEADS].T
            o = _dsa_attention(q, qi, wit, k, v, ki2, topk=min(TOPK_MAX, S // 4))
            xs = _matmul_residual(o, attn_w_out, xs, j)
        else:
            a = _conv_proj(xs, conv_norm, conv_w_in, conv_w, j)
            xs = _matmul_residual(a, conv_w_out, xs, j)
        xs = _ffn(xs, ffn_norm, ffn_w_gate, ffn_w_up, ffn_w_down, i)
    return xs[None]
```

```python
import functools

import jax
import jax.numpy as jnp
import numpy as np
from jax import lax
from jax.experimental import pallas as pl
from jax.experimental.pallas import tpu as pltpu

HEAD_DIM = 128
N_KV_HEADS = 4
IDX_HEADS = 16
IDX_DIM = 64
TOPK_MAX = 256
CONV_WIDTH = 3
ROPE_THETA = 10000.0
RMS_EPS = 1e-6

LANES = 128
VMEM_LIMIT_BYTES = 56 * 1024 * 1024
CONV_VMEM_LIMIT_BYTES = 60 * 1024 * 1024

ROW_TILE = 512
COL_TILE = 512
FFN_ROW_TILE = 1024
FFN_HIDDEN_TILE = 256
Q_TILE = 256
KEY_CHUNK = 512
COUNT_ROWS = 64

NEG_BIAS = -1e30
LOG2_E = 1.4426950408889634
BOUND_SLACK = 1.01
DENOM_MIN = 2.0 ** -100
I16_MIN, I16_MAX = -2 ** 15, 2 ** 15 - 1

_NT = (((1,), (1,)), ((), ()))

f32 = jnp.float32
bf16 = jnp.bfloat16
i32 = jnp.int32
i16 = jnp.int16


def _resident(block_shape, index_map):
    return pl.BlockSpec(block_shape, index_map, pipeline_mode=pl.Buffered(1))


def _layer_resident(stacked, layer):
    zeros = (0,) * (stacked.ndim - 1)
    return _resident((None,) + stacked.shape[1:], lambda *_: (layer,) + zeros)


def _params(*semantics):
    return pltpu.CompilerParams(dimension_semantics=semantics, vmem_limit_bytes=VMEM_LIMIT_BYTES)


def _rmsnorm_bf16(x, g):
    ms = jnp.mean(x * x, axis=-1, keepdims=True)
    return (x * lax.rsqrt(ms + RMS_EPS) * g).astype(bf16)


def _rope_pairs(y, cos, sin, half):
    if 2 * half == LANES:
        partner = pltpu.roll(y, half, 1)
    else:
        lane = lax.broadcasted_iota(i32, y.shape, 1)
        partner = jnp.where((lane & (2 * half - 1)) < half,
                            pltpu.roll(y, LANES - half, 1), pltpu.roll(y, half, 1))
    return y * cos + partner * sin


def _attn_proj_kernel(x_ref, g_ref, w_ref, qg_ref, kg_ref, cosh_ref, sinh_ref,
                      cosi_ref, sini_ref,
                      q_ref, k_ref, v_ref, qi_ref, ki2_ref, wi_ref,
                      *, q_cols, kv_cols, qi_cols, q_scale, wi_scale):
    h = _rmsnorm_bf16(x_ref[...], g_ref[...])
    cosh, sinh = cosh_ref[...], sinh_ref[...]
    cosi, sini = cosi_ref[...], sini_ref[...]

    def head_norm_rope(acc, gain, scale, out_ref, col0):
        for t in range(acc.shape[1] // HEAD_DIM):
            y = acc[:, t * HEAD_DIM:(t + 1) * HEAD_DIM]
            ms = jnp.mean(y * y, axis=-1, keepdims=True)
            y = y * lax.rsqrt(ms + RMS_EPS) * gain
            y = _rope_pairs(y, cosh, sinh, HEAD_DIM // 2)
            if scale != 1.0:
                y = y * scale
            out_ref[:, col0 + t * HEAD_DIM:col0 + (t + 1) * HEAD_DIM] = y.astype(out_ref.dtype)

    col = 0
    for c0 in range(0, q_cols, COL_TILE):
        acc = jnp.dot(h, w_ref[:, col + c0:col + c0 + COL_TILE], preferred_element_type=f32)
        head_norm_rope(acc, qg_ref[...], q_scale, q_ref, c0)
    col += q_cols
    for c0 in range(0, kv_cols, COL_TILE):
        acc = jnp.dot(h, w_ref[:, col + c0:col + c0 + COL_TILE], preferred_element_type=f32)
        head_norm_rope(acc, kg_ref[...], 1.0, k_ref, c0)
    col += kv_cols
    for c0 in range(0, kv_cols, COL_TILE):
        acc = jnp.dot(h, w_ref[:, col + c0:col + c0 + COL_TILE], preferred_element_type=f32)
        v_ref[:, c0:c0 + COL_TILE] = acc.astype(v_ref.dtype)
    col += kv_cols
    for c0 in range(0, qi_cols, COL_TILE):
        acc = jnp.dot(h, w_ref[:, col + c0:col + c0 + COL_TILE], preferred_element_type=f32)
        for t in range(COL_TILE // LANES):
            y = _rope_pairs(acc[:, t * LANES:(t + 1) * LANES], cosi, sini, IDX_DIM // 2)
            qi_ref[:, c0 + t * LANES:c0 + (t + 1) * LANES] = y.astype(qi_ref.dtype)
    col += qi_cols
    acc = jnp.dot(h, w_ref[:, col:col + LANES], preferred_element_type=f32)
    ki = _rope_pairs(acc, cosi, sini, IDX_DIM // 2)
    lane = lax.broadcasted_iota(i32, acc.shape, 1)
    ki2_ref[...] = jnp.where(lane < IDX_DIM, ki, pltpu.roll(ki, IDX_DIM, 1)).astype(ki2_ref.dtype)
    wi_ref[...] = acc * wi_scale


def _attn_proj(x, g, w_all, qg, kg, tables, layer, *, q_cols, kv_cols, qi_cols):
    S, D = x.shape
    tm = ROW_TILE
    cosh, sinh, cosi, sini = tables
    row = lambda i: (i, 0)
    kern = functools.partial(_attn_proj_kernel, q_cols=q_cols, kv_cols=kv_cols, qi_cols=qi_cols,
                             q_scale=HEAD_DIM ** -0.5 * LOG2_E,
                             wi_scale=IDX_HEADS ** -0.5 * IDX_DIM ** -0.5)
    return pl.pallas_call(
        kern,
        grid=(S // tm,),
        in_specs=[
            pl.BlockSpec((tm, D), row),
            _layer_resident(g, layer),
            _layer_resident(w_all, layer),
            _layer_resident(qg, layer),
            _layer_resident(kg, layer),
            pl.BlockSpec((tm, LANES), row), pl.BlockSpec((tm, LANES), row),
            pl.BlockSpec((tm, LANES), row), pl.BlockSpec((tm, LANES), row),
        ],
        out_specs=[
            pl.BlockSpec((tm, q_cols), row),
            pl.BlockSpec((tm, kv_cols), row),
            pl.BlockSpec((tm, kv_cols), row),
            pl.BlockSpec((tm, qi_cols), row),
            pl.BlockSpec((tm, LANES), row),
            pl.BlockSpec((tm, LANES), row),
        ],
        out_shape=[
            jax.ShapeDtypeStruct((S, q_cols), bf16),
            jax.ShapeDtypeStruct((S, kv_cols), bf16),
            jax.ShapeDtypeStruct((S, kv_cols), bf16),
            jax.ShapeDtypeStruct((S, qi_cols), bf16),
            jax.ShapeDtypeStruct((S, LANES), bf16),
            jax.ShapeDtypeStruct((S, LANES), f32),
        ],
        compiler_params=_params("parallel"),
        name="attn_proj",
    )(x, g, w_all, qg, kg, cosh, sinh, cosi, sini)


def _dsa_attention_kernel(q_ref, qi_ref, wit_ref, k_ref, v_ref, ki2_ref, o_ref,
                          key_ref, hi_ref, lo_ref, qcat_ref, p_ref, acc_ref, kmax_ref,
                          *, topk, n_heads, idx_bits):
    tq, kc = Q_TILE, KEY_CHUNK
    q0 = pl.program_id(0) * tq
    n_chunks = lax.div(q0 + tq + kc - 1, kc)
    qpos = q0 + lax.broadcasted_iota(i32, (1, tq), 1)
    n_rep = n_heads // N_KV_HEADS
    neg_inf_key = jnp.int32(-2139095041)

    def score_chunk(c, carry):
        r0 = pl.multiple_of(c * kc, kc)
        ki2 = ki2_ref[pl.ds(r0, kc), :]
        first_head = lax.broadcasted_iota(i32, ki2.shape, 1) < IDX_DIM
        ka = jnp.where(first_head, ki2, jnp.zeros_like(ki2))
        kb = jnp.where(first_head, jnp.zeros_like(ki2), ki2)
        acc = jnp.zeros((kc, tq), f32)
        for p in range(IDX_HEADS // 2):
            rhs = qi_ref[:, p * LANES:(p + 1) * LANES]
            sa = lax.dot_general(ka, rhs, _NT, preferred_element_type=f32)
            sb = lax.dot_general(kb, rhs, _NT, preferred_element_type=f32)
            acc = acc + jnp.maximum(sa, 0.0) * wit_ref[2 * p:2 * p + 1, :]
            acc = acc + jnp.maximum(sb, 0.0) * wit_ref[2 * p + 1:2 * p + 2, :]
        kpos = r0 + lax.broadcasted_iota(i32, (kc, 1), 0)
        acc = jnp.where(kpos <= qpos, acc, -jnp.inf)
        bits = pltpu.bitcast(acc, i32)
        key = bits ^ ((bits >> 31) & jnp.int32(0x7FFFFFFF))
        key_ref[pl.ds(r0, kc), :] = key
        hi_ref[pl.ds(r0, kc), :] = (key >> 16).astype(i16)
        lo_ref[pl.ds(r0, kc), :] = ((key & 0xFFFF) + I16_MIN).astype(i16)
        return carry

    lax.fori_loop(0, n_chunks, score_chunk, 0)

    def count_where(pred):
        def body(c, cnt):
            r0 = pl.multiple_of(c * kc, kc)
            hit = jnp.where(pred(key_ref[pl.ds(r0, kc), :], r0), 1, 0).astype(i32)
            return cnt + hit.reshape(kc // COUNT_ROWS, COUNT_ROWS, tq).sum(axis=0)
        cnt = lax.fori_loop(0, n_chunks, body, jnp.zeros((COUNT_ROWS, tq), i32))
        return cnt.sum(axis=0, keepdims=True)

    def count_ge16(ref, cand):
        cand16 = cand.astype(i16)

        def body(c, cnt):
            r0 = pl.multiple_of(c * kc, kc)
            hit = jnp.where(ref[pl.ds(r0, kc), :] >= cand16, jnp.int16(1), jnp.int16(0))
            for t in range(kc // COUNT_ROWS):
                cnt = cnt + hit[t * COUNT_ROWS:(t + 1) * COUNT_ROWS]
            return cnt
        cnt = lax.fori_loop(0, n_chunks, body, jnp.zeros((COUNT_ROWS, tq), i16))
        return cnt.astype(i32).sum(axis=0, keepdims=True)

    def search16(ref, cnt_all):
        def bit(it, carry):
            t, cnt_ge = carry
            cand = t ^ jnp.left_shift(jnp.int32(1), 15 - it)
            cand = jnp.where(it == 0, jnp.zeros_like(cand), cand)
            tot = count_ge16(ref, cand)
            take = tot >= topk
            return jnp.where(take, cand, t), jnp.where(take, tot, cnt_ge)
        return lax.fori_loop(0, 16, bit, (jnp.full((1, tq), I16_MIN, i32), cnt_all))

    prefix, cnt_hi = search16(hi_ref, jnp.full((1, tq), n_chunks * kc, i32))
    prefix16 = prefix.astype(i16)

    def pin_chunk(c, carry):
        r0 = pl.multiple_of(c * kc, kc)
        hi = hi_ref[pl.ds(r0, kc), :]
        lo = lo_ref[pl.ds(r0, kc), :]
        lo_ref[pl.ds(r0, kc), :] = jnp.where(hi > prefix16, jnp.int16(I16_MAX),
                                             jnp.where(hi < prefix16, jnp.int16(I16_MIN), lo))
        return carry

    lax.fori_loop(0, n_chunks, pin_chunk, 0)
    low, cnt_ge = search16(lo_ref, cnt_hi)
    thr = prefix * 65536 + (low - I16_MIN)

    def no_ties():
        return jnp.full((1, tq), jnp.iinfo(jnp.int32).max, i32)

    def break_ties():
        n_eq = count_where(lambda blk, r0: blk == thr)
        want = topk - (cnt_ge - n_eq)

        def index_bit(it, cut):
            cand = cut | jnp.left_shift(jnp.int32(1), idx_bits - 1 - it)

            def pred(blk, r0):
                kpos = r0 + lax.broadcasted_iota(i32, (kc, 1), 0)
                return (blk == thr) & (kpos < cand)
            return jnp.where(count_where(pred) < want, cand, cut)

        return lax.fori_loop(0, idx_bits, index_bit, jnp.zeros((1, tq), i32))

    idx_cut = lax.cond(jnp.max(cnt_ge) > topk, break_ties, no_ties)

    @pl.when(pl.program_id(0) == 0)
    def _():
        def knorm_chunk(c, best):
            r0 = pl.multiple_of(c * kc, kc)
            kf = k_ref[pl.ds(r0, kc), :].astype(f32)
            sq = kf * kf
            for g in range(N_KV_HEADS):
                n2 = jnp.sum(sq[:, g * HEAD_DIM:(g + 1) * HEAD_DIM], axis=1, keepdims=True)
                best = jnp.maximum(best, jnp.max(n2, axis=0, keepdims=True))
            return best

        best = lax.fori_loop(0, k_ref.shape[0] // kc, knorm_chunk, jnp.zeros((1, 1), f32))
        kmax_ref[...] = jnp.broadcast_to(best, kmax_ref.shape)

    ones = jnp.ones((8, HEAD_DIM), f32)
    qn2 = jnp.zeros((8, tq), f32)
    for h in range(n_heads):
        qf = q_ref[:, h * HEAD_DIM:(h + 1) * HEAD_DIM].astype(f32)
        qn2 = jnp.maximum(qn2, lax.dot_general(ones, qf * qf, _NT, preferred_element_type=f32))
    m_fix = jnp.sqrt(qn2[0:1, :] * kmax_ref[0:1, :]) * BOUND_SLACK

    ones_col = (lax.broadcasted_iota(i32, (kc, HEAD_DIM), 1) == 0).astype(bf16)
    for g in range(N_KV_HEADS):
        for r in range(n_rep):
            h = g * n_rep + r
            qcat_ref[g, r * tq:(r + 1) * tq, :] = q_ref[:, h * HEAD_DIM:(h + 1) * HEAD_DIM]

    def mask_t(c):
        r0 = pl.multiple_of(c * kc, kc)
        blk = key_ref[pl.ds(r0, kc), :]
        kpos = r0 + lax.broadcasted_iota(i32, (kc, 1), 0)
        sel = ((blk > thr) | ((blk == thr) & (kpos <= idx_cut))) & (blk > neg_inf_key)
        return jnp.where(sel, -m_fix, NEG_BIAS).T

    def group_logits(c, g, bias_t):
        r0 = pl.multiple_of(c * kc, kc)
        k_c = k_ref[pl.ds(r0, kc), g * HEAD_DIM:(g + 1) * HEAD_DIM]
        s = lax.dot_general(qcat_ref[g], k_c, _NT, preferred_element_type=f32)
        return (s.reshape(n_rep, tq, kc) + bias_t[None]).reshape(n_rep * tq, kc)

    def group_values(c, g):
        r0 = pl.multiple_of(c * kc, kc)
        return jnp.concatenate([v_ref[pl.ds(r0, kc), g * HEAD_DIM:(g + 1) * HEAD_DIM], ones_col],
                               axis=1)

    def write_group(g):
        acc = acc_ref[g]
        denom = acc[:, HEAD_DIM:HEAD_DIM + 1]
        o = acc[:, :HEAD_DIM] * (1.0 / denom)
        for r in range(n_rep):
            h = g * n_rep + r
            o_ref[:, h * HEAD_DIM:(h + 1) * HEAD_DIM] = o[r * tq:(r + 1) * tq].astype(o_ref.dtype)
        return jnp.min(denom)

    acc_ref[...] = jnp.zeros_like(acc_ref)

    def probabilities(c):
        bias_t = mask_t(c)
        for g in range(N_KV_HEADS):
            p_ref[g] = jnp.exp2(group_logits(c, g, bias_t)).astype(bf16)

    def weighted_values(c):
        for g in range(N_KV_HEADS):
            acc_ref[g] += jnp.dot(p_ref[g], group_values(c, g), preferred_element_type=f32)

    def attend_chunk(c, carry):
        weighted_values(c - 1)
        probabilities(c)
        return carry

    probabilities(0)
    lax.fori_loop(1, n_chunks, attend_chunk, 0)
    weighted_values(n_chunks - 1)
    denom_min = write_group(0)
    for g in range(1, N_KV_HEADS):
        denom_min = jnp.minimum(denom_min, write_group(g))

    @pl.when(jnp.logical_not(denom_min >= DENOM_MIN))
    def _():
        acc_ref[...] = jnp.zeros_like(acc_ref)

        def attend_online(c, ms):
            bias_t = mask_t(c)
            new_ms = []
            for g in range(N_KV_HEADS):
                s = group_logits(c, g, bias_t)
                m_new = jnp.maximum(ms[g], s.max(axis=1, keepdims=True))
                p = jnp.exp2(s - m_new).astype(bf16)
                acc_ref[g] = (jnp.exp2(ms[g] - m_new) * acc_ref[g]
                              + jnp.dot(p, group_values(c, g), preferred_element_type=f32))
                new_ms.append(m_new)
            return tuple(new_ms)

        lax.fori_loop(0, n_chunks, attend_online,
                      tuple(jnp.full((n_rep * tq, 1), NEG_BIAS, f32) for _ in range(N_KV_HEADS)))
        for g in range(N_KV_HEADS):
            write_group(g)


def _dsa_attention(q, qi, wit, k, v, ki2, *, topk):
    S, q_cols = q.shape
    n_heads = q_cols // HEAD_DIM
    n_rep = n_heads // N_KV_HEADS
    kv_cols = k.shape[1]
    tq, kc = Q_TILE, KEY_CHUNK
    row = lambda i: (i, 0)
    const = lambda i: (0, 0)
    kern = functools.partial(_dsa_attention_kernel, topk=topk, n_heads=n_heads,
                             idx_bits=max(1, (S - 1).bit_length()))
    return pl.pallas_call(
        kern,
        grid=(S // tq,),
        in_specs=[
            pl.BlockSpec((tq, q_cols), row),
            pl.BlockSpec((tq, qi.shape[1]), row),
            pl.BlockSpec((IDX_HEADS, tq), lambda i: (0, i)),
            _resident((S, kv_cols), const),
            _resident((S, kv_cols), const),
            _resident((S, LANES), const),
        ],
        out_specs=pl.BlockSpec((tq, q_cols), row),
        out_shape=jax.ShapeDtypeStruct((S, q_cols), bf16),
        scratch_shapes=[pltpu.VMEM((S, tq), i32),
                        pltpu.VMEM((S, tq), i16),
                        pltpu.VMEM((S, tq), i16),
                        pltpu.VMEM((N_KV_HEADS, n_rep * tq, HEAD_DIM), bf16),
                        pltpu.VMEM((N_KV_HEADS, n_rep * tq, kc), bf16),
                        pltpu.VMEM((N_KV_HEADS, n_rep * tq, 2 * HEAD_DIM), f32),
                        pltpu.VMEM((8, tq), f32)],
        compiler_params=_params("arbitrary"),
        name="dsa_attention",
    )(q, qi, wit, k, v, ki2)


def _matmul_residual_kernel(a_ref, w_ref, res_ref, o_ref):
    a = a_ref[...]
    for c0 in range(0, o_ref.shape[1], COL_TILE):
        acc = jnp.dot(a, w_ref[:, c0:c0 + COL_TILE], preferred_element_type=f32)
        o_ref[:, c0:c0 + COL_TILE] = res_ref[:, c0:c0 + COL_TILE] + acc


def _matmul_residual(a, w, res, layer):
    S, K = a.shape
    N = w.shape[2]
    tm = ROW_TILE
    row = lambda i: (i, 0)
    return pl.pallas_call(
        _matmul_residual_kernel,
        grid=(S // tm,),
        in_specs=[pl.BlockSpec((tm, K), row), _layer_resident(w, layer),
                  pl.BlockSpec((tm, N), row)],
        out_specs=pl.BlockSpec((tm, N), row),
        out_shape=jax.ShapeDtypeStruct((S, N), f32),
        compiler_params=_params("parallel"),
        name="matmul_residual",
    )(a, w, res)


def _conv_proj_kernel(x_ref, g_ref, w_ref, cw_ref, wo_ref, o_ref, tail_ref, a_ref):
    tm, D = x_ref.shape

    @pl.when(pl.program_id(0) == 0)
    def _():
        tail_ref[...] = jnp.zeros_like(tail_ref)

    h = _rmsnorm_bf16(x_ref[...], g_ref[...])
    rowi = lax.broadcasted_iota(i32, (tm, COL_TILE), 0)
    for c0 in range(0, D, COL_TILE):
        cols = slice(c0, c0 + COL_TILE)
        b = jnp.dot(h, w_ref[:, c0:c0 + COL_TILE], preferred_element_type=f32)
        c = jnp.dot(h, w_ref[:, D + c0:D + c0 + COL_TILE], preferred_element_type=f32)
        u = jnp.dot(h, w_ref[:, 2 * D + c0:2 * D + c0 + COL_TILE], preferred_element_type=f32)
        z = c * u
        prev1 = tail_ref[7:8, cols]
        prev2 = tail_ref[6:7, cols]
        z1 = jnp.where(rowi == 0, prev1, pltpu.roll(z, 1, 0))
        z2 = jnp.where(rowi == 0, prev2, jnp.where(rowi == 1, prev1, pltpu.roll(z, 2, 0)))
        y = cw_ref[0:1, cols] * z2 + cw_ref[1:2, cols] * z1 + cw_ref[2:3, cols] * z
        a_ref[:, cols] = (b * y).astype(a_ref.dtype)
        tail_ref[:, cols] = z[tm - 8:tm, :]
    a = a_ref[...]
    for c0 in range(0, D, COL_TILE):
        cols = slice(c0, c0 + COL_TILE)
        o_ref[:, cols] = x_ref[:, cols] + jnp.dot(a, wo_ref[:, cols], preferred_element_type=f32)


def _conv_layer(x, g, w_in, conv_w, w_out, layer):
    S, D = x.shape
    tm = ROW_TILE
    row = lambda i: (i, 0)
    return pl.pallas_call(
        _conv_proj_kernel,
        grid=(S // tm,),
        in_specs=[pl.BlockSpec((tm, D), row), _layer_resident(g, layer),
                  _layer_resident(w_in, layer), _layer_resident(conv_w, layer),
                  _layer_resident(w_out, layer)],
        out_specs=pl.BlockSpec((tm, D), row),
        out_shape=jax.ShapeDtypeStruct((S, D), f32),
        scratch_shapes=[pltpu.VMEM((8, D), f32), pltpu.VMEM((tm, D), bf16)],
        compiler_params=pltpu.CompilerParams(
            dimension_semantics=("arbitrary",), vmem_limit_bytes=CONV_VMEM_LIMIT_BYTES),
        name="conv_layer",
    )(x, g, w_in, conv_w, w_out)


def _ffn_kernel(x_ref, g_ref, wg_ref, wu_ref, wd_ref, o_ref, h_ref):
    @pl.when(pl.program_id(1) == 0)
    def _():
        x = x_ref[...]
        h_ref[...] = _rmsnorm_bf16(x, g_ref[...])
        o_ref[...] = x

    h = h_ref[...]
    gate = jnp.dot(h, wg_ref[...].astype(bf16), preferred_element_type=f32)
    up = jnp.dot(h, wu_ref[...].astype(bf16), preferred_element_type=f32)
    act = (gate * jax.nn.sigmoid(gate) * up).astype(bf16)
    wd = wd_ref[...].astype(bf16)
    for c0 in range(0, o_ref.shape[1], COL_TILE):
        o_ref[:, c0:c0 + COL_TILE] += jnp.dot(act, wd[:, c0:c0 + COL_TILE],
                                              preferred_element_type=f32)


def _ffn(x, g, wg, wu, wd, layer):
    S, D = x.shape
    H = wg.shape[2]
    tm, th = FFN_ROW_TILE, FFN_HIDDEN_TILE
    return pl.pallas_call(
        _ffn_kernel,
        grid=(S // tm, H // th),
        in_specs=[pl.BlockSpec((tm, D), lambda i, j: (i, 0), pipeline_mode=pl.Buffered(1)),
                  _layer_resident(g, layer),
                  pl.BlockSpec((None, D, th), lambda i, j: (layer, 0, j)),
                  pl.BlockSpec((None, D, th), lambda i, j: (layer, 0, j)),
                  pl.BlockSpec((None, th, D), lambda i, j: (layer, j, 0))],
        out_specs=pl.BlockSpec((tm, D), lambda i, j: (i, 0)),
        out_shape=jax.ShapeDtypeStruct((S, D), f32),
        scratch_shapes=[pltpu.VMEM((tm, D), bf16)],
        compiler_params=_params("parallel", "arbitrary"),
        name="ffn",
    )(x, g, wg, wu, wd)


def _rope_tables(seq_len):
    def cos_sin(dim):
        inv_freq = (1.0 / (np.float32(ROPE_THETA) ** (np.arange(0, dim, 2, dtype=np.float32) / dim)))
        ang = np.arange(seq_len, dtype=np.float32)[:, None] * inv_freq.astype(np.float32)[None, :]
        return np.cos(ang).astype(np.float32), np.sin(ang).astype(np.float32)

    ch, sh = (jnp.asarray(t) for t in cos_sin(HEAD_DIM))
    ci, si = (jnp.asarray(t) for t in cos_sin(IDX_DIM))
    cosh = jnp.concatenate([ch, ch], axis=-1)
    sinh = jnp.concatenate([-sh, sh], axis=-1)
    cosi = jnp.concatenate([ci] * 4, axis=-1)
    sini = jnp.concatenate([-si, si] * 2, axis=-1)
    return cosh, sinh, cosi, sini


def kernel(x, attn_norm, attn_w_in, attn_q_norm, attn_k_norm, attn_w_out, conv_norm, conv_w_in,
           conv_w, conv_w_out, ffn_norm, ffn_w_gate, ffn_w_up, ffn_w_down):
    B, S, D = x.shape
    assert B == 1 and S % max(KEY_CHUNK, FFN_ROW_TILE) == 0 and D % COL_TILE == 0
    assert ffn_w_gate.shape[2] % FFN_HIDDEN_TILE == 0
    depth = ffn_norm.shape[0]
    tables = _rope_tables(S)
    q_cols, kv_cols, qi_cols = D, N_KV_HEADS * HEAD_DIM, IDX_HEADS * IDX_DIM

    row_vec = lambda p: p[:, None, :]
    pad = (-attn_w_in.shape[2]) % LANES
    attn_w_all = jnp.pad(attn_w_in, ((0, 0), (0, 0), (0, pad))).astype(bf16)
    attn_w_out, conv_w_in, conv_w_out = (w.astype(bf16) for w in (attn_w_out, conv_w_in, conv_w_out))
    attn_norm, attn_q_norm, attn_k_norm, conv_norm, ffn_norm = (
        row_vec(p) for p in (attn_norm, attn_q_norm, attn_k_norm, conv_norm, ffn_norm))

    xs = x[0]
    for i in range(depth):
        j = i // 2
        if i % 2 == 0:
            q, k, v, qi, ki2, wi = _attn_proj(
                xs, attn_norm, attn_w_all, attn_q_norm, attn_k_norm, tables, j,
                q_cols=q_cols, kv_cols=kv_cols, qi_cols=qi_cols)
            wit = wi[:, IDX_DIM:IDX_DIM + IDX_HEADS].T
            o = _dsa_attention(q, qi, wit, k, v, ki2, topk=min(TOPK_MAX, S // 4))
            xs = _matmul_residual(o, attn_w_out, xs, j)
        else:
            xs = _conv_layer(xs, conv_norm, conv_w_in, conv_w, conv_w_out, j)
        xs = _ffn(xs, ffn_norm, ffn_w_gate, ffn_w_up, ffn_w_down, i)
    return xs[None]
```
